```python
import math
import jax, jax.numpy as jnp
from jax import lax
import numpy as np

D_MODEL = 2048
BATCH = 16
SEQ = 256
DEPTH = 1
DEC_BATCH = 8
DEC_SEQ = 1024
PAST_LEN = 256

GRID_W = 64
D_MIX = D_MODEL
D_RET = D_MIX // 2
H_RET = 8
DH = D_RET // H_RET
D_FOUR = D_MIX - D_RET
N_FGROUPS = 4
FG = D_FOUR // N_FGROUPS
D_IN = 4 * D_RET + D_FOUR
CHUNK = 128
ROPE_THETA = 10000.0
N_EXPERTS = 64
TOP_K = 8
D_EXPERT = 512
D_SHARED = 512
N_GROUPS = 8
TOPK_GROUP = 4
ROUTED_SCALE = 2.5
MOE_BLOCK = 128
ALPHA = (2.0 * DEPTH) ** 0.25
BETA = (8.0 * DEPTH) ** -0.25
LN_EPS = 1e-6
GN_EPS = 1e-5

kernel_name = "hybrid_retention_fnet_moe_diffusion_step"

F32 = jnp.float32


def _layernorm(x, w=None, b=None):
    xf = x.astype(F32)
    mu = xf.mean(-1, keepdims=True)
    var = jnp.square(xf - mu).mean(-1, keepdims=True)
    y = (xf - mu) * lax.rsqrt(var + LN_EPS)
    if w is not None:
        y = y * w.astype(F32) + b.astype(F32)
    return y.astype(x.dtype)


def _axial_rope(x):
    B, L, H, Dh = x.shape
    rows = L // GRID_W
    row = jnp.repeat(jnp.arange(rows, dtype=F32), GRID_W)
    col = jnp.tile(jnp.arange(GRID_W, dtype=F32), rows)
    nf = Dh // 4
    freqs = ROPE_THETA ** (-jnp.arange(nf, dtype=F32) / nf)

    def rot(t, pos):
        ang = pos[:, None] * freqs[None, :]
        cos = jnp.cos(ang)[None, :, None, :]
        sin = jnp.sin(ang)[None, :, None, :]
        t1, t2 = t[..., :nf], t[..., nf:]
        return jnp.concatenate([t1 * cos - t2 * sin, t1 * sin + t2 * cos], axis=-1)

    xf = x.astype(F32)
    out = jnp.concatenate([rot(xf[..., : Dh // 2], row), rot(xf[..., Dh // 2:], col)], axis=-1)
    return out.astype(x.dtype)


def _retention_chunkwise(q, k, v, log_g, s0):
    B, L, H, Dh = q.shape
    n = L // CHUNK

    def to_chunks(t):
        return t.astype(F32).reshape(B, n, CHUNK, H, Dh).transpose(1, 0, 3, 2, 4)

    qc, kc, vc = to_chunks(q), to_chunks(k), to_chunks(v)
    idx = jnp.arange(CHUNK, dtype=F32)
    diff = idx[:, None] - idx[None, :]
    dmask = jnp.exp(jnp.where(diff[None] >= 0, diff[None] * log_g[:, None, None], -jnp.inf))
    q_dec = jnp.exp((idx + 1.0)[None, :] * log_g[:, None])
    k_dec = jnp.exp((CHUNK - 1.0 - idx)[None, :] * log_g[:, None])
    c_dec = jnp.exp(CHUNK * log_g)

    def step(s, inp):
        qb, kb, vb = inp
        scores = jnp.einsum('bhid,bhjd->bhij', qb, kb) * dmask
        o = jnp.einsum('bhij,bhjd->bhid', scores, vb)
        o = o + jnp.einsum('bhid,bhde->bhie', qb * q_dec[:, :, None], s)
        s = s * c_dec[:, None, None] + jnp.einsum('bhjd,bhje->bhde', kb * k_dec[:, :, None], vb)
        return s, o

    s, o = lax.scan(step, s0.astype(F32), (qc, kc, vc))
    o = o.transpose(1, 0, 3, 2, 4).reshape(B, L, H, Dh)
    return o, s


def _fourier_mix(f):
    B, L, _ = f.shape
    fg = f.astype(F32).reshape(B, L, N_FGROUPS, FG)
    out = jnp.fft.fftn(fg, axes=(1, 3), norm='ortho').real
    return out.reshape(B, L, D_FOUR).astype(f.dtype)


def _mixer(h, s0, is_latent, w_in, ret_decay, ret_gn_w, w_out):
    B, L, _ = h.shape
    z = h @ w_in
    q = z[..., :D_RET].reshape(B, L, H_RET, DH)
    k = (z[..., D_RET:2 * D_RET] * (DH ** -0.5)).reshape(B, L, H_RET, DH)
    v = z[..., 2 * D_RET:3 * D_RET].reshape(B, L, H_RET, DH)
    g = z[..., 3 * D_RET:4 * D_RET]
    f = z[..., 4 * D_RET:]
    if is_latent:
        q = _axial_rope(q)
        k = _axial_rope(k)
    log_g = jax.nn.log_sigmoid(ret_decay.astype(F32))
    o_f, s_f = _retention_chunkwise(q, k, v, log_g[0], s0[:, 0])
    o_b, s_b = _retention_chunkwise(jnp.flip(q, 1), jnp.flip(k, 1), jnp.flip(v, 1), log_g[1], s0[:, 1])
    o = o_f + jnp.flip(o_b, 1)
    mu = o.mean(-1, keepdims=True)
    var = jnp.square(o - mu).mean(-1, keepdims=True)
    o = ((o - mu) * lax.rsqrt(var + GN_EPS)).reshape(B, L, D_RET) * ret_gn_w.astype(F32)
    ret_out = (jax.nn.silu(g.astype(F32)) * o).astype(h.dtype)
    four_out = _fourier_mix(f)
    mix = jnp.concatenate([ret_out, four_out], axis=-1) @ w_out
    return mix, jnp.stack([s_f, s_b], axis=1)


def _moe(h, w_router, router_bias, w_gate_e, w_up_e, w_down_e, w_gate_s, w_up_s, w_down_s):
    T, D = h.shape
    scores = jax.nn.sigmoid(h.astype(F32) @ w_router.astype(F32))
    biased = scores + router_bias.astype(F32)
    grp = biased.reshape(T, N_GROUPS, N_EXPERTS // N_GROUPS)
    grp_score = lax.top_k(grp, 2)[0].sum(-1)
    _, gidx = lax.top_k(grp_score, TOPK_GROUP)
    gmask = jax.nn.one_hot(gidx, N_GROUPS, dtype=F32).sum(1)
    emask = jnp.repeat(gmask, N_EXPERTS // N_GROUPS, axis=1)
    _, eidx = lax.top_k(jnp.where(emask > 0, biased, -jnp.inf), TOP_K)
    w = jnp.take_along_axis(scores, eidx, axis=1)
    w = w / w.sum(-1, keepdims=True) * ROUTED_SCALE

    n_assign = T * TOP_K
    flat_e = eidx.reshape(-1)
    order = jnp.argsort(flat_e)
    sorted_e = flat_e[order]
    token_of = order // TOP_K
    w_sorted = w.reshape(-1)[order]
    counts = jnp.bincount(flat_e, length=N_EXPERTS)
    start = jnp.cumsum(counts) - counts
    pcounts = (counts + MOE_BLOCK - 1) // MOE_BLOCK * MOE_BLOCK
    pend = jnp.cumsum(pcounts)
    pstart = pend - pcounts
    pos = pstart[sorted_e] + jnp.arange(n_assign) - start[sorted_e]
    n_pad = -(-n_assign // MOE_BLOCK) * MOE_BLOCK + N_EXPERTS * MOE_BLOCK
    n_blocks = n_pad // MOE_BLOCK
    block_e = jnp.minimum(jnp.searchsorted(pend, jnp.arange(n_blocks) * MOE_BLOCK, side='right'), N_EXPERTS - 1)
    xbuf = jnp.zeros((n_pad, D), h.dtype).at[pos].set(h[token_of])

    def expert_block(args):
        xb, e = args
        return (jax.nn.silu(xb @ w_gate_e[e]) * (xb @ w_up_e[e])) @ w_down_e[e]

    ybuf = lax.map(expert_block, (xbuf.reshape(n_blocks, MOE_BLOCK, D), block_e)).reshape(n_pad, D)
    routed = jnp.zeros((T, D), F32).at[token_of].add(ybuf[pos].astype(F32) * w_sorted[:, None])
    shared = (jax.nn.silu(h @ w_gate_s) * (h @ w_up_s)) @ w_down_s
    return (routed + shared.astype(F32)).astype(h.dtype)


def _layer(x, mod, s0, is_latent, p):
    B, L, D = x.shape
    shift1, scale1, gate1, shift2, scale2, gate2 = [m[:, None, :] for m in jnp.split(mod, 6, axis=-1)]
    h = (_layernorm(x) * (1 + scale1) + shift1).astype(x.dtype)
    mix, st = _mixer(h, s0, is_latent, p['w_in'], p['ret_decay'], p['ret_gn_w'], p['w_out'])
    x = _layernorm(ALPHA * x + gate1 * mix, p['ln1_w'], p['ln1_b'])
    h = (_layernorm(x) * (1 + scale2) + shift2).astype(x.dtype)
    m = _moe(h.reshape(B * L, D), p['w_router'], p['router_bias'], p['w_gate_e'], p['w_up_e'],
             p['w_down_e'], p['w_gate_s'], p['w_up_s'], p['w_down_s']).reshape(B, L, D)
    x = _layernorm(ALPHA * x + gate2 * m, p['ln2_w'], p['ln2_b'])
    return x, st


def setup_inputs(seed: int = 0) -> dict:
    key = jax.random.key(seed)
    ks = jax.random.split(key, 24)

    def nrm(k, shape, scale):
        return jax.random.normal(k, shape, F32) * scale

    w_in = nrm(ks[5], (DEPTH, D_MODEL, D_IN), D_MODEL ** -0.5)
    col_scale = jnp.concatenate([jnp.ones((2 * D_RET,), F32), jnp.full((D_RET,), BETA, F32),
                                 jnp.ones((D_RET,), F32), jnp.full((D_FOUR,), BETA, F32)])
    w_in = w_in * col_scale
    base_decay = 1.0 - 2.0 ** (-5.0 - jnp.arange(H_RET, dtype=F32))
    decay_logit = jnp.log(base_decay / (1.0 - base_decay))
    ret_decay = decay_logit[None, None, :] + nrm(ks[6], (DEPTH, 2, H_RET), 0.1)
    return {
        'x_prompt': nrm(ks[0], (BATCH, SEQ, D_MODEL), 1.0),
        'x_sample': nrm(ks[1], (DEC_BATCH, DEC_SEQ, D_MODEL), 1.0),
        'c': nrm(ks[2], (DEC_BATCH, D_MODEL), 1.0),
        'state_ret': nrm(ks[3], (DEC_BATCH, DEPTH, 2, H_RET, DH, DH), 0.5),
        'c_ctx': nrm(ks[4], (D_MODEL,), 1.0),
        'w_ada': nrm(ks[7], (DEPTH, D_MODEL, 6 * D_MODEL), 0.5 * D_MODEL ** -0.5),
        'b_ada': nrm(ks[8], (DEPTH, 6 * D_MODEL), 0.02),
        'w_in': w_in,
        'ret_decay': ret_decay,
        'ret_gn_w': 1.0 + nrm(ks[9], (DEPTH, D_RET), 0.02),
        'w_out': nrm(ks[10], (DEPTH, D_MIX, D_MODEL), BETA * D_MIX ** -0.5),
        'ln1_w': 1.0 + nrm(ks[11], (DEPTH, D_MODEL), 0.02),
        'ln1_b': nrm(ks[12], (DEPTH, D_MODEL), 0.02),
        'w_router': nrm(ks[13], (DEPTH, D_MODEL, N_EXPERTS), D_MODEL ** -0.5),
        'router_bias': nrm(ks[14], (DEPTH, N_EXPERTS), 0.01),
        'w_gate_e': nrm(ks[15], (DEPTH, N_EXPERTS, D_MODEL, D_EXPERT), D_MODEL ** -0.5),
        'w_up_e': nrm(ks[16], (DEPTH, N_EXPERTS, D_MODEL, D_EXPERT), D_MODEL ** -0.5),
        'w_down_e': nrm(ks[17], (DEPTH, N_EXPERTS, D_EXPERT, D_MODEL), BETA * D_EXPERT ** -0.5),
        'w_gate_s': nrm(ks[18], (DEPTH, D_MODEL, D_SHARED), D_MODEL ** -0.5),
        'w_up_s': nrm(ks[19], (DEPTH, D_MODEL, D_SHARED), D_MODEL ** -0.5),
        'w_down_s': nrm(ks[20], (DEPTH, D_SHARED, D_MODEL), BETA * D_SHARED ** -0.5),
        'ln2_w': 1.0 + nrm(ks[21], (DEPTH, D_MODEL), 0.02),
        'ln2_b': nrm(ks[22], (DEPTH, D_MODEL), 0.02),
    }


def reference(x_prompt, x_sample, c, state_ret, c_ctx, w_ada, b_ada, w_in, ret_decay, ret_gn_w,
              w_out, ln1_w, ln1_b, w_router, router_bias, w_gate_e, w_up_e, w_down_e,
              w_gate_s, w_up_s, w_down_s, ln2_w, ln2_b):
    y_prompt = x_prompt
    y_sample = x_sample
    ctx_states = []
    for l in range(DEPTH):
        p = {'w_in': w_in[l], 'ret_decay': ret_decay[l], 'ret_gn_w': ret_gn_w[l], 'w_out': w_out[l],
             'ln1_w': ln1_w[l], 'ln1_b': ln1_b[l], 'w_router': w_router[l], 'router_bias': router_bias[l],
             'w_gate_e': w_gate_e[l], 'w_up_e': w_up_e[l], 'w_down_e': w_down_e[l],
             'w_gate_s': w_gate_s[l], 'w_up_s': w_up_s[l], 'w_down_s': w_down_s[l],
             'ln2_w': ln2_w[l], 'ln2_b': ln2_b[l]}
        mod_ctx = (jax.nn.silu(c_ctx) @ w_ada[l] + b_ada[l])[None, :]
        mod_lat = jax.nn.silu(c) @ w_ada[l] + b_ada[l]
        s0_ctx = jnp.zeros((y_prompt.shape[0], 2, H_RET, DH, DH), F32)
        y_prompt, st = _layer(y_prompt, mod_ctx, s0_ctx, False, p)
        ctx_states.append(st.astype(x_prompt.dtype))
        y_sample, _ = _layer(y_sample, mod_lat, state_ret[:, l].astype(F32), True, p)
    new_state_ret = jnp.stack(ctx_states, axis=1)
    return (y_prompt, y_sample, new_state_ret)
```

```python
import functools

import numpy as np
import jax
import jax.numpy as jnp
from jax import lax
from jax.experimental import pallas as pl
from jax.experimental.pallas import tpu as pltpu

F32 = jnp.float32
BF16 = jnp.bfloat16
I32 = jnp.int32

GRID_W = 64
H_RET = 8
N_FGROUPS = 4
CHUNK = 128
ROPE_THETA = 10000.0
N_EXPERTS = 64
TOP_K = 8
N_GROUPS = 8
TOPK_GROUP = 4
ROUTED_SCALE = 2.5
LN_EPS = 1e-6
GN_EPS = 1e-5

VMEM_LIMIT_BYTES = 56 * 1024 * 1024

ROW_TILE = 1024
IN_COL_TILE = 512
OUT_COL_TILE = 512
OUT_ROW_TILE = 512
ADA_COL_TILE = 1024
ROUTER_TILE = 512
DISPATCH_TILE = 256
EXPERT_ROWS = 256
COMBINE_TILE = 128
LN_ROWS = 256


def _params(*sem):
    return pltpu.CompilerParams(dimension_semantics=sem, vmem_limit_bytes=VMEM_LIMIT_BYTES)


def _ln(x):
    mu = jnp.mean(x, axis=-1, keepdims=True)
    xc = x - mu
    var = jnp.mean(xc * xc, axis=-1, keepdims=True)
    return xc * lax.rsqrt(var + LN_EPS)


def _silu(x):
    return x * jax.nn.sigmoid(x)


def _dot(a, b):
    return jnp.dot(a, b, preferred_element_type=F32)


def _dot_nt(a, b):
    return lax.dot_general(a, b, (((1,), (1,)), ((), ())), preferred_element_type=F32)


def _dot_tn(a, b):
    return lax.dot_general(a, b, (((0,), (0,)), ((), ())), preferred_element_type=F32)


def _ada_kernel(c_ref, w_ref, b_ref, o_ref):
    s = _silu(c_ref[...]).astype(BF16)
    o_ref[...] = _dot(s, w_ref[...].astype(BF16)) + b_ref[...]


def _ada(cc, w_ada, b_ada):
    rows, d = cc.shape
    n = w_ada.shape[1]
    return pl.pallas_call(
        _ada_kernel,
        grid=(n // ADA_COL_TILE,),
        in_specs=[
            pl.BlockSpec((rows, d), lambda j: (0, 0)),
            pl.BlockSpec((d, ADA_COL_TILE), lambda j: (0, j)),
            pl.BlockSpec((1, ADA_COL_TILE), lambda j: (0, j)),
        ],
        out_specs=pl.BlockSpec((rows, ADA_COL_TILE), lambda j: (0, j)),
        out_shape=jax.ShapeDtypeStruct((rows, n), F32),
        compiler_params=_params("arbitrary"),
        name="ada",
    )(cc, w_ada, b_ada)


def _mod_row(i, tile, n_prompt_rows, dec_seq):
    return jnp.where(i * tile < n_prompt_rows, 0, 1 + (i * tile - n_prompt_rows) // dec_seq)


def _inproj_kernel(x_ref, mod_ref, w_ref, z_ref, h_scr, *, d):
    @pl.when(pl.program_id(1) == 0)
    def _():
        shift = mod_ref[0, :, 0:d]
        scale = mod_ref[0, :, d:2 * d]
        for r in range(0, x_ref.shape[0], LN_ROWS):
            rows = slice(r, r + LN_ROWS)
            h_scr[rows, :] = (_ln(x_ref[rows, :]) * (1.0 + scale) + shift).astype(BF16)

    z_ref[...] = _dot(h_scr[...], w_ref[...].astype(BF16)).astype(BF16)


def _inproj(x, mod3, w_in, n_prompt_rows, dec_seq):
    t, d = x.shape
    n = w_in.shape[1]
    mod_map = functools.partial(_mod_row, tile=ROW_TILE, n_prompt_rows=n_prompt_rows, dec_seq=dec_seq)
    return pl.pallas_call(
        functools.partial(_inproj_kernel, d=d),
        grid=(t // ROW_TILE, n // IN_COL_TILE),
        in_specs=[
            pl.BlockSpec((ROW_TILE, d), lambda i, j: (i, 0)),
            pl.BlockSpec((1, 1, mod3.shape[2]), lambda i, j: (mod_map(i), 0, 0)),
            pl.BlockSpec((d, IN_COL_TILE), lambda i, j: (0, j)),
        ],
        out_specs=pl.BlockSpec((ROW_TILE, IN_COL_TILE), lambda i, j: (i, j)),
        out_shape=jax.ShapeDtypeStruct((t, n), BF16),
        scratch_shapes=[pltpu.VMEM((ROW_TILE, d), BF16)],
        compiler_params=_params("arbitrary", "arbitrary"),
        name="inproj",
    )(x, mod3, w_in)


def _rope_tables(seq):
    dh = CHUNK
    nf = dh // 4
    pos = np.arange(seq)
    row = (pos // GRID_W).astype(np.float64)
    col = (pos % GRID_W).astype(np.float64)
    freqs = ROPE_THETA ** (-np.arange(nf, dtype=np.float64) / nf)
    ar = row[:, None] * freqs[None, :]
    ac = col[:, None] * freqs[None, :]
    cos = np.concatenate([np.cos(ar), np.cos(ar), np.cos(ac), np.cos(ac)], axis=1)
    sin = np.concatenate([-np.sin(ar), np.sin(ar), -np.sin(ac), np.sin(ac)], axis=1)
    return jnp.asarray(cos, F32), jnp.asarray(sin, F32)


def _retention_kernel(*refs, seq, latent, k_scale):
    if latent:
        (lg_ref, q_ref, k_ref, v_ref, g_ref, gnw_ref, cos_ref, sin_ref, s0_ref,
         o_ref, q_scr, k_scr, o_scr) = refs
        st_ref = None
    else:
        (lg_ref, q_ref, k_ref, v_ref, g_ref, gnw_ref,
         o_ref, st_ref, q_scr, k_scr, o_scr) = refs
        s0_ref = cos_ref = sin_ref = None
    c = CHUNK
    n = seq // c
    h = pl.program_id(1)
    lgf = lg_ref[0, h]
    lgb = lg_ref[1, h]

    ii = lax.broadcasted_iota(I32, (c, c), 0).astype(F32)
    jj = lax.broadcasted_iota(I32, (c, c), 1).astype(F32)
    diff = ii - jj
    neg_inf = jnp.float32(-jnp.inf)
    dmask = (jnp.exp(jnp.where(diff >= 0, diff * lgf, neg_inf))
             + jnp.exp(jnp.where(diff <= 0, (-diff) * lgb, neg_inf)))
    idx = lax.broadcasted_iota(I32, (c, 1), 0).astype(F32)
    qdec_f = jnp.exp((idx + 1.0) * lgf)
    kdec_f = jnp.exp((c - 1.0 - idx) * lgf)
    qdec_b = jnp.exp((c - idx) * lgb)
    kdec_b = jnp.exp(idx * lgb)
    ones_row = jnp.ones((1, c), F32)
    cdec_f = jnp.exp(ones_row * (c * lgf))
    cdec_b = jnp.exp(ones_row * (c * lgb))

    if latent:
        lane = lax.broadcasted_iota(I32, (c, c), 1)
        first_quarter = (lane % (c // 2)) < (c // 4)

        def rope(x, rows):
            partner = jnp.where(first_quarter, pltpu.roll(x, c - c // 4, 1), pltpu.roll(x, c // 4, 1))
            return x * cos_ref[rows, :] + partner * sin_ref[rows, :]

    for ci in range(n):
        rows = slice(ci * c, (ci + 1) * c)
        q = q_ref[rows, :].astype(F32)
        k = k_ref[rows, :].astype(F32) * k_scale
        if latent:
            q = rope(q, rows)
            k = rope(k, rows)
        q_scr[rows, :] = q
        k_scr[rows, :] = k
        scores = _dot_nt(q.astype(BF16), k.astype(BF16)) * dmask
        o_scr[rows, :] = _dot(scores.astype(BF16), v_ref[rows, :])

    s = s0_ref[0] if latent else jnp.zeros((c, c), F32)
    for ci in range(n):
        rows = slice(ci * c, (ci + 1) * c)
        q = q_scr[rows, :]
        k = k_scr[rows, :]
        o_scr[rows, :] += _dot((q * qdec_f).astype(BF16), s.astype(BF16))
        s = s * cdec_f + _dot_tn((k * kdec_f).astype(BF16), v_ref[rows, :])
    if st_ref is not None:
        st_ref[0] = s

    s = s0_ref[1] if latent else jnp.zeros((c, c), F32)
    gnw = gnw_ref[...]
    for ci in reversed(range(n)):
        rows = slice(ci * c, (ci + 1) * c)
        q = q_scr[rows, :]
        k = k_scr[rows, :]
        o = o_scr[rows, :] + _dot((q * qdec_b).astype(BF16), s.astype(BF16))
        s = s * cdec_b + _dot_tn((k * kdec_b).astype(BF16), v_ref[rows, :])
        mu = jnp.mean(o, axis=-1, keepdims=True)
        oc = o - mu
        var = jnp.mean(oc * oc, axis=-1, keepdims=True)
        on = oc * lax.rsqrt(var + GN_EPS) * gnw
        o_ref[rows, :] = (_silu(g_ref[rows, :].astype(F32)) * on).astype(BF16)
    if st_ref is not None:
        st_ref[1] = s


def _retention(z, log_g, gn_w, *, batch, seq, row_block_offset, latent, state=None):
    dh = CHUNK
    d_ret = H_RET * dh
    zspec = lambda off: pl.BlockSpec((seq, dh), lambda b, h: (row_block_offset + b, off + h))
    in_specs = [
        pl.BlockSpec(memory_space=pltpu.SMEM),
        zspec(0), zspec(H_RET), zspec(2 * H_RET), zspec(3 * H_RET),
        pl.BlockSpec((1, dh), lambda b, h: (0, h)),
    ]
    args = [log_g, z, z, z, z, gn_w]
    out_specs = [pl.BlockSpec((seq, dh), lambda b, h: (b, h))]
    out_shape = [jax.ShapeDtypeStruct((batch * seq, d_ret), BF16)]
    state_spec = pl.BlockSpec((None, 2, None, dh, dh), lambda b, h: (b, 0, h, 0, 0))
    if latent:
        cos, sin = _rope_tables(seq)
        in_specs += [pl.BlockSpec((seq, dh), lambda b, h: (0, 0)),
                     pl.BlockSpec((seq, dh), lambda b, h: (0, 0)),
                     state_spec]
        args += [cos, sin, state]
    else:
        out_specs.append(state_spec)
        out_shape.append(jax.ShapeDtypeStruct((batch, 2, H_RET, dh, dh), F32))
    return pl.pallas_call(
        functools.partial(_retention_kernel, seq=seq, latent=latent, k_scale=dh ** -0.5),
        grid=(batch, H_RET),
        in_specs=in_specs,
        out_specs=out_specs,
        out_shape=out_shape,
        scratch_shapes=[pltpu.VMEM((seq, dh), F32), pltpu.VMEM((seq, dh), F32), pltpu.VMEM((seq, dh), F32)],
        compiler_params=_params("arbitrary", "arbitrary"),
        name="retention_latent" if latent else "retention_context",
    )(*args)


def _dft_tables(n):
    jk = np.outer(np.arange(n), np.arange(n)) % n
    ang = 2.0 * np.pi * jk / n
    return np.cos(ang) / np.sqrt(n), np.sin(ang) / np.sqrt(n)


def _fourier_kernel(x_ref, cc_ref, sc_ref, cl_ref, sl_ref, o_ref):
    x = x_ref[...]
    a = _dot(x, cc_ref[...]).astype(BF16)
    b = _dot(x, sc_ref[...]).astype(BF16)
    o_ref[...] = (_dot(cl_ref[...], a) - _dot(sl_ref[...], b)).astype(BF16)


def _fourier(z, *, batch, seq, row_block_offset, col_block_offset, fg):
    cl, sl = _dft_tables(seq)
    cc, sc = _dft_tables(fg)
    tables = [jnp.asarray(t, BF16) for t in (cc, sc, cl, sl)]
    const = lambda n: pl.BlockSpec((n, n), lambda b, g: (0, 0))
    return pl.pallas_call(
        _fourier_kernel,
        grid=(batch, N_FGROUPS),
        in_specs=[pl.BlockSpec((seq, fg), lambda b, g: (row_block_offset + b, col_block_offset + g)),
                  const(fg), const(fg), const(seq), const(seq)],
        out_specs=pl.BlockSpec((seq, fg), lambda b, g: (b, g)),
        out_shape=jax.ShapeDtypeStruct((batch * seq, N_FGROUPS * fg), BF16),
        compiler_params=_params("arbitrary", "arbitrary"),
        name=f"fourier_{seq}",
    )(z, *tables)


def _outproj_kernel(ret_ref, four_ref, w_ref, x_ref, mod_ref, lnw_ref, lnb_ref, wr_ref,
                    x1_ref, h2_ref, lg_ref, acc, *, d, d_ret, alpha):
    j = pl.program_id(1)
    w = w_ref[...].astype(BF16)
    part = _dot(ret_ref[...], w[:d_ret]) + _dot(four_ref[...], w[d_ret:])
    for jj in range(d // OUT_COL_TILE):
        @pl.when(j == jj)
        def _(jj=jj):
            acc[:, jj * OUT_COL_TILE:(jj + 1) * OUT_COL_TILE] = part

    @pl.when(j == pl.num_programs(1) - 1)
    def _():
        gate1 = mod_ref[0, :, 2 * d:3 * d]
        shift2 = mod_ref[0, :, 3 * d:4 * d]
        scale2 = mod_ref[0, :, 4 * d:5 * d]
        wr = wr_ref[...]
        w_hi = wr.astype(BF16)
        w_lo = (wr - w_hi.astype(F32)).astype(BF16)
        for r in range(0, x_ref.shape[0], LN_ROWS):
            rows = slice(r, r + LN_ROWS)
            x1 = _ln(alpha * x_ref[rows, :] + gate1 * acc[rows, :]) * lnw_ref[...] + lnb_ref[...]
            x1_ref[rows, :] = x1
            h2 = _ln(x1) * (1.0 + scale2) + shift2
            h2_ref[rows, :] = h2
            h_hi = h2.astype(BF16)
            h_lo = (h2 - h_hi.astype(F32)).astype(BF16)
            lg_ref[:, rows] = _dot_nt(w_hi, h_hi) + (_dot_nt(w_hi, h_lo) + _dot_nt(w_lo, h_hi))


def _outproj(ret, four, w_out, x, mod3, ln_w, ln_b, w_router_t, n_prompt_rows, dec_seq, alpha):
    t, d = x.shape
    d_ret = ret.shape[1]
    tm = OUT_ROW_TILE
    mod_map = functools.partial(_mod_row, tile=tm, n_prompt_rows=n_prompt_rows, dec_seq=dec_seq)
    n_e = w_router_t.shape[0]
    return pl.pallas_call(
        functools.partial(_outproj_kernel, d=d, d_ret=d_ret, alpha=alpha),
        grid=(t // tm, d // OUT_COL_TILE),
        in_specs=[
            pl.BlockSpec((tm, d_ret), lambda i, j: (i, 0)),
            pl.BlockSpec((tm, four.shape[1]), lambda i, j: (i, 0)),
            pl.BlockSpec((w_out.shape[0], OUT_COL_TILE), lambda i, j: (0, j)),
            pl.BlockSpec((tm, d), lambda i, j: (i, 0)),
            pl.BlockSpec((1, 1, mod3.shape[2]), lambda i, j: (mod_map(i), 0, 0)),
            pl.BlockSpec((1, d), lambda i, j: (0, 0)),
            pl.BlockSpec((1, d), lambda i, j: (0, 0)),
            pl.BlockSpec((n_e, d), lambda i, j: (0, 0)),
        ],
        out_specs=[
            pl.BlockSpec((tm, d), lambda i, j: (i, 0)),
            pl.BlockSpec((tm, d), lambda i, j: (i, 0)),
            pl.BlockSpec((n_e, tm), lambda i, j: (0, i)),
        ],
        out_shape=[
            jax.ShapeDtypeStruct((t, d), F32),
            jax.ShapeDtypeStruct((t, d), F32),
            jax.ShapeDtypeStruct((n_e, t), F32),
        ],
        scratch_shapes=[pltpu.VMEM((tm, d), F32)],
        compiler_params=_params("arbitrary", "arbitrary"),
        name="outproj",
    )(ret, four, w_out, x, mod3, ln_w, ln_b, w_router_t)


def _first_index_of_max(vals, index, sentinel, axis):
    m = jnp.max(vals, axis=axis, keepdims=True)
    return jnp.min(jnp.where(vals == m, index, sentinel), axis=axis, keepdims=True), m


def _router_kernel(lg_ref, bias_ref, eidx_ref, w_ref, rank_ref, cnt_ref, carry):
    i = pl.program_id(0)
    tr = lg_ref.shape[1]
    gsz = N_EXPERTS // N_GROUPS
    neg_inf = jnp.float32(-jnp.inf)

    @pl.when(i == 0)
    def _():
        carry[...] = jnp.zeros_like(carry)

    scores = jax.nn.sigmoid(lg_ref[...])
    biased = scores + bias_ref[...]
    b3 = biased.reshape(N_GROUPS, gsz, tr)
    sub = lax.broadcasted_iota(I32, (N_GROUPS, gsz, tr), 1)
    first, m1 = _first_index_of_max(b3, sub, gsz, 1)
    m2 = jnp.max(jnp.where(sub == first, neg_inf, b3), axis=1, keepdims=True)
    gscore = (m1 + m2).reshape(N_GROUPS, tr)

    gi = lax.broadcasted_iota(I32, (N_GROUPS, tr), 0)
    gsel = jnp.zeros((N_GROUPS, tr), jnp.bool_)
    cur = gscore
    for _ in range(TOPK_GROUP):
        first, _m = _first_index_of_max(cur, gi, N_GROUPS, 0)
        pick = gi == first
        gsel = jnp.logical_or(gsel, pick)
        cur = jnp.where(pick, neg_inf, cur)

    gsel3 = jnp.broadcast_to(gsel.reshape(N_GROUPS, 1, tr), (N_GROUPS, gsz, tr))
    cur = jnp.where(gsel3, b3, neg_inf).reshape(N_EXPERTS, tr)
    ei = lax.broadcasted_iota(I32, (N_EXPERTS, tr), 0)
    picks = []
    sel_scores = []
    for _ in range(TOP_K):
        first, _m = _first_index_of_max(cur, ei, N_EXPERTS, 0)
        pick = ei == first
        picks.append(first)
        sel_scores.append(jnp.sum(jnp.where(pick, scores, 0.0), axis=0, keepdims=True))
        cur = jnp.where(pick, neg_inf, cur)
    total = sel_scores[0]
    for sc in sel_scores[1:]:
        total = total + sc

    member = jnp.zeros((N_EXPERTS, tr), F32)
    for first in picks:
        member = member + jnp.where(ei == first, 1.0, 0.0)
    src = lax.broadcasted_iota(I32, (tr, tr), 0)
    dst = lax.broadcasted_iota(I32, (tr, tr), 1)
    upper = jnp.where(src < dst, 1.0, 0.0).astype(BF16)
    rank_all = _dot(member.astype(BF16), upper) + carry[...]
    for k in range(TOP_K):
        pick = ei == picks[k]
        eidx_ref[k:k + 1, :] = picks[k]
        w_ref[k:k + 1, :] = sel_scores[k] / total * ROUTED_SCALE
        rank_ref[k:k + 1, :] = jnp.sum(jnp.where(pick, rank_all, 0.0), axis=0, keepdims=True).astype(I32)
    carry[...] = carry[...] + jnp.sum(member, axis=1, keepdims=True)
    cnt_ref[...] = carry[...]


def _router(logits_t, bias):
    n_e, t = logits_t.shape
    tr = ROUTER_TILE
    tok = pl.BlockSpec((TOP_K, tr), lambda i: (0, i))
    return pl.pallas_call(
        _router_kernel,
        grid=(t // tr,),
        in_specs=[pl.BlockSpec((n_e, tr), lambda i: (0, i)),
                  pl.BlockSpec((n_e, 1), lambda i: (0, 0))],
        out_specs=[tok, tok, tok, pl.BlockSpec((n_e, 1), lambda i: (0, 0))],
        out_shape=[jax.ShapeDtypeStruct((TOP_K, t), I32),
                   jax.ShapeDtypeStruct((TOP_K, t), F32),
                   jax.ShapeDtypeStruct((TOP_K, t), I32),
                   jax.ShapeDtypeStruct((n_e, 1), F32)],
        scratch_shapes=[pltpu.VMEM((n_e, 1), F32)],
        compiler_params=_params("arbitrary"),
        name="router",
    )(logits_t, bias)


def _row_copy(src_ref, src_row, dst_ref, dst_row, sem):
    return pltpu.make_async_copy(src_ref.at[pl.ds(src_row, 1)], dst_ref.at[pl.ds(dst_row, 1)], sem)


def _dispatch_kernel(pos_ref, h_ref, xs_ref, sem):
    td = h_ref.shape[0]

    def issue(t, carry):
        for k in range(TOP_K):
            _row_copy(h_ref, t, xs_ref, pos_ref[k, t], sem).start()
        return carry

    lax.fori_loop(0, td, issue, 0)

    def drain(t, carry):
        for k in range(TOP_K):
            _row_copy(h_ref, t, xs_ref, pos_ref[k, t], sem).wait()
        return carry

    lax.fori_loop(0, td, drain, 0)


def _dispatch(pos, h2):
    t, d = h2.shape
    td = DISPATCH_TILE
    return pl.pallas_call(
        _dispatch_kernel,
        grid=(t // td,),
        in_specs=[pl.BlockSpec((TOP_K, td), lambda i: (0, i), memory_space=pltpu.SMEM),
                  pl.BlockSpec((td, d), lambda i: (i, 0))],
        out_specs=pl.BlockSpec(memory_space=pl.ANY),
        out_shape=jax.ShapeDtypeStruct((t * TOP_K, d), F32),
        scratch_shapes=[pltpu.SemaphoreType.DMA],
        compiler_params=_params("arbitrary"),
        name="dispatch",
    )(pos, h2)


def _experts_kernel(blk_ref, exp_ref, lo_ref, hi_ref, np_ref, x_ref, wg_ref, wu_ref, wd_ref,
                    y_ref, wg_s, wu_s, wd_s):
    s = pl.program_id(0)
    prev = jnp.maximum(s - 1, 0)
    valid = s < np_ref[0]
    new_expert = jnp.logical_or(s == 0, exp_ref[s] != exp_ref[prev])
    first_visit = jnp.logical_or(s == 0, blk_ref[s] != blk_ref[prev])
    tm = x_ref.shape[0]

    @pl.when(jnp.logical_and(valid, new_expert))
    def _():
        wg_s[...] = wg_ref[0].astype(BF16)
        wu_s[...] = wu_ref[0].astype(BF16)
        wd_s[...] = wd_ref[0].astype(BF16)

    def compute():
        x = x_ref[...].astype(BF16)
        a = (_silu(_dot(x, wg_s[...])) * _dot(x, wu_s[...])).astype(BF16)
        y = _dot(a, wd_s[...])
        rows = blk_ref[s] * tm + lax.broadcasted_iota(I32, (tm, 1), 0)
        mine = jnp.logical_and(rows >= lo_ref[s], rows < hi_ref[s])
        return y, mine

    @pl.when(jnp.logical_and(valid, first_visit))
    def _():
        y, mine = compute()
        y_ref[...] = jnp.where(mine, y, 0.0)

    @pl.when(jnp.logical_and(valid, jnp.logical_not(first_visit)))
    def _():
        y, mine = compute()
        y_ref[...] = jnp.where(mine, y, y_ref[...])


def _experts(meta, xs, w_gate_e, w_up_e, w_down_e):
    blk, exp, lo, hi, n_pairs = meta
    n_rows, d = xs.shape
    d_e = w_gate_e.shape[2]
    tm = EXPERT_ROWS
    n_steps = blk.shape[0]
    grid_spec = pltpu.PrefetchScalarGridSpec(
        num_scalar_prefetch=5,
        grid=(n_steps,),
        in_specs=[
            pl.BlockSpec((tm, d), lambda s, blk, exp, lo, hi, n: (blk[s], 0)),
            pl.BlockSpec((1, d, d_e), lambda s, blk, exp, lo, hi, n: (exp[s], 0, 0)),
            pl.BlockSpec((1, d, d_e), lambda s, blk, exp, lo, hi, n: (exp[s], 0, 0)),
            pl.BlockSpec((1, d_e, d), lambda s, blk, exp, lo, hi, n: (exp[s], 0, 0)),
        ],
        out_specs=pl.BlockSpec((tm, d), lambda s, blk, exp, lo, hi, n: (blk[s], 0)),
        scratch_shapes=[pltpu.VMEM((d, d_e), BF16), pltpu.VMEM((d, d_e), BF16), pltpu.VMEM((d_e, d), BF16)],
    )
    return pl.pallas_call(
        _experts_kernel,
        grid_spec=grid_spec,
        out_shape=jax.ShapeDtypeStruct((n_rows, d), F32),
        compiler_params=_params("arbitrary"),
        name="experts",
    )(blk, exp, lo, hi, n_pairs, xs, w_gate_e, w_up_e, w_down_e)


def _expert_schedule(counts, n_rows):
    tm = EXPERT_ROWS
    n_blocks = n_rows // tm
    n_steps = n_blocks + N_EXPERTS - 1
    end = jnp.cumsum(counts)
    start = end - counts
    first_blk = start // tm
    last_blk = (end - 1) // tm
    n_blk = jnp.where(counts > 0, last_blk - first_blk + 1, 0)
    pair_end = jnp.cumsum(n_blk)
    pair_start = pair_end - n_blk
    n_pairs = pair_end[-1]
    step = jnp.minimum(jnp.arange(n_steps, dtype=I32), n_pairs - 1)
    exp = jnp.searchsorted(pair_end, step, side="right").astype(I32)
    blk = (first_blk[exp] + step - pair_start[exp]).astype(I32)
    return blk, exp, start[exp].astype(I32), end[exp].astype(I32), n_pairs.reshape(1).astype(I32), start


def _cast_kernel(x_ref, o_ref):
    o_ref[...] = x_ref[...].astype(o_ref.dtype)


def _cast_bf16(w):
    r, c = w.shape
    tile = min(r, 512)
    return pl.pallas_call(
        _cast_kernel,
        grid=(r // tile,),
        in_specs=[pl.BlockSpec((tile, c), lambda i: (i, 0))],
        out_specs=pl.BlockSpec((tile, c), lambda i: (i, 0)),
        out_shape=jax.ShapeDtypeStruct((r, c), BF16),
        compiler_params=_params("arbitrary"),
        name="cast_bf16",
    )(w)


def _combine_kernel(pos_ref, w_ref, h_ref, x1_ref, mod_ref, lnw_ref, lnb_ref, wg_ref, wu_ref, wd_ref,
                    y_ref, o_ref, gbuf, sem, *, d, alpha):
    tc = h_ref.shape[0]

    def issue(t, carry):
        for k in range(TOP_K):
            _row_copy(y_ref, pos_ref[k, t], gbuf.at[k], t, sem).start()
        return carry

    lax.fori_loop(0, tc, issue, 0)

    hb = h_ref[...].astype(BF16)
    a = (_silu(_dot(hb, wg_ref[...])) * _dot(hb, wu_ref[...])).astype(BF16)
    m = _dot(a, wd_ref[...])

    def drain(t, carry):
        for k in range(TOP_K):
            _row_copy(y_ref, pos_ref[k, t], gbuf.at[k], t, sem).wait()
        return carry

    lax.fori_loop(0, tc, drain, 0)

    w = w_ref[...]
    for k in range(TOP_K):
        m = m + gbuf[k] * w[:, k:k + 1]
    gate2 = mod_ref[0, :, 5 * d:6 * d]
    o_ref[...] = _ln(alpha * x1_ref[...] + gate2 * m) * lnw_ref[...] + lnb_ref[...]


def _combine(pos, wts, h2, x1, mod3, ln_w, ln_b, wg, wu, wd, ybuf, n_prompt_rows, dec_seq, alpha):
    t, d = h2.shape
    tc = COMBINE_TILE
    mod_map = functools.partial(_mod_row, tile=tc, n_prompt_rows=n_prompt_rows, dec_seq=dec_seq)
    const = lambda a: pl.BlockSpec(a.shape, lambda i: (0, 0))
    return pl.pallas_call(
        functools.partial(_combine_kernel, d=d, alpha=alpha),
        grid=(t // tc,),
        in_specs=[
            pl.BlockSpec((TOP_K, tc), lambda i: (0, i), memory_space=pltpu.SMEM),
            pl.BlockSpec((tc, TOP_K), lambda i: (i, 0)),
            pl.BlockSpec((tc, d), lambda i: (i, 0)),
            pl.BlockSpec((tc, d), lambda i: (i, 0)),
            pl.BlockSpec((1, 1, mod3.shape[2]), lambda i: (mod_map(i), 0, 0)),
            const(ln_w), const(ln_b), const(wg), const(wu), const(wd),
            pl.BlockSpec(memory_space=pl.ANY),
        ],
        out_specs=pl.BlockSpec((tc, d), lambda i: (i, 0)),
        out_shape=jax.ShapeDtypeStruct((t, d), F32),
        scratch_shapes=[pltpu.VMEM((TOP_K, tc, d), F32), pltpu.SemaphoreType.DMA],
        compiler_params=_params("arbitrary"),
        name="combine",
    )(pos, wts, h2, x1, mod3, ln_w, ln_b, wg, wu, wd, ybuf)


def kernel(x_prompt, x_sample, c, state_ret, c_ctx, w_ada, b_ada, w_in, ret_decay, ret_gn_w, w_out,
           ln1_w, ln1_b, w_router, router_bias, w_gate_e, w_up_e, w_down_e, w_gate_s, w_up_s, w_down_s,
           ln2_w, ln2_b):
    batch, seq, d = x_prompt.shape
    dec_batch, dec_seq, _ = x_sample.shape
    depth = w_in.shape[0]
    assert depth == 1
    n_prompt_rows = batch * seq
    alpha = (2.0 * depth) ** 0.25
    d_ret = H_RET * CHUNK
    fg = (w_in.shape[2] - 4 * d_ret) // N_FGROUPS
    l = 0

    x = jnp.concatenate([x_prompt.reshape(n_prompt_rows, d), x_sample.reshape(dec_batch * dec_seq, d)], axis=0)
    t = x.shape[0]

    n_mod = 1 + dec_batch
    cc = jnp.concatenate([c_ctx[None, :], c, jnp.zeros((-n_mod % 8, d), F32)], axis=0)
    mod = _ada(cc, w_ada[l], b_ada[l][None, :])
    mod3 = mod.reshape(mod.shape[0], 1, mod.shape[1])

    z = _inproj(x, mod3, w_in[l], n_prompt_rows, dec_seq)

    log_g = jax.nn.log_sigmoid(ret_decay[l].astype(F32))
    gn_w = ret_gn_w[l][None, :]
    ret_p, ctx_state = _retention(z, log_g, gn_w, batch=batch, seq=seq, row_block_offset=0, latent=False)
    ret_s = _retention(z, log_g, gn_w, batch=dec_batch, seq=dec_seq, row_block_offset=n_prompt_rows // dec_seq,
                       latent=True, state=state_ret[:, l])[0]
    fcol = 4 * d_ret // fg
    four_p = _fourier(z, batch=batch, seq=seq, row_block_offset=0, col_block_offset=fcol, fg=fg)
    four_s = _fourier(z, batch=dec_batch, seq=dec_seq, row_block_offset=n_prompt_rows // dec_seq,
                      col_block_offset=fcol, fg=fg)
    ret = jnp.concatenate([ret_p, ret_s], axis=0)
    four = jnp.concatenate([four_p, four_s], axis=0)

    x1, h2, logits_t = _outproj(ret, four, w_out[l], x, mod3, ln1_w[l][None, :], ln1_b[l][None, :],
                                w_router[l].T, n_prompt_rows, dec_seq, alpha)

    eidx, wts, rank, counts = _router(logits_t, router_bias[l][:, None])
    counts = counts[:, 0].astype(I32)
    blk, exp, lo, hi, n_pairs, start = _expert_schedule(counts, t * TOP_K)
    pos = start[eidx] + rank

    xs = _dispatch(pos, h2)
    ybuf = _experts((blk, exp, lo, hi, n_pairs), xs, w_gate_e[l], w_up_e[l], w_down_e[l])
    out = _combine(pos, wts.T, h2, x1, mod3, ln2_w[l][None, :], ln2_b[l][None, :],
                   _cast_bf16(w_gate_s[l]), _cast_bf16(w_up_s[l]), _cast_bf16(w_down_s[l]),
                   ybuf, n_prompt_rows, dec_seq, alpha)

    y_prompt = out[:n_prompt_rows].reshape(batch, seq, d)
    y_sample = out[n_prompt_rows:].reshape(dec_batch, dec_seq, d)
    new_state = ctx_state[:, None].astype(x_prompt.dtype)
    return (y_prompt, y_sample, new_state)
```

```python
import functools

import numpy as np
import jax
import jax.numpy as jnp
from jax import lax
from jax.experimental import pallas as pl
from jax.experimental.pallas import tpu as pltpu

F32 = jnp.float32
BF16 = jnp.bfloat16
I32 = jnp.int32
U32 = jnp.uint32

GRID_W = 64
H_RET = 8
N_FGROUPS = 4
CHUNK = 128
ROPE_THETA = 10000.0
N_EXPERTS = 64
TOP_K = 8
N_GROUPS = 8
TOPK_GROUP = 4
ROUTED_SCALE = 2.5
LN_EPS = 1e-6
GN_EPS = 1e-5

VMEM_LIMIT_BYTES = 56 * 1024 * 1024

ROW_TILE = 1024
IN_COL_TILE = 512
OUT_ROW_TILE = 256
ADA_COL_TILE = 1024
ROUTER_TILE = 512
DISPATCH_TILE = 256
EXPERT_ROWS = 256
COMBINE_TILE = 256
LN_ROWS = 256


def _params(*sem):
    return pltpu.CompilerParams(dimension_semantics=sem, vmem_limit_bytes=VMEM_LIMIT_BYTES)


def _ln(x):
    mu = jnp.mean(x, axis=-1, keepdims=True)
    xc = x - mu
    var = jnp.mean(xc * xc, axis=-1, keepdims=True)
    return xc * lax.rsqrt(var + LN_EPS)


def _silu(x):
    return x * jax.nn.sigmoid(x)


def _dot(a, b):
    return jnp.dot(a, b, preferred_element_type=F32)


def _dot_nt(a, b):
    return lax.dot_general(a, b, (((1,), (1,)), ((), ())), preferred_element_type=F32)


def _dot_tn(a, b):
    return lax.dot_general(a, b, (((0,), (0,)), ((), ())), preferred_element_type=F32)


def _pack_halves(x):
    n = x.shape[1] // 2
    lo = lax.bitcast_convert_type(x[:, :n].astype(BF16).astype(F32), U32)
    hi = lax.bitcast_convert_type(x[:, n:].astype(BF16).astype(F32), U32)
    return (lo >> 16) | hi


def _unpack_halves(u):
    lo = lax.bitcast_convert_type(u << 16, F32)
    hi = lax.bitcast_convert_type(u & jnp.uint32(0xFFFF0000), F32)
    return lo, hi


def _dot_halves(lo, hi, w_ref):
    n = lo.shape[1]
    return _dot(lo, w_ref[0:n, :]) + _dot(hi, w_ref[n:2 * n, :])


def _ada_kernel(c_ref, w_ref, b_ref, o_ref):
    s = _silu(c_ref[...]).astype(BF16)
    o_ref[...] = _dot(s, w_ref[...].astype(BF16)) + b_ref[...]


def _ada(cc, w_ada, b_ada):
    rows, d = cc.shape
    n = w_ada.shape[1]
    return pl.pallas_call(
        _ada_kernel,
        grid=(n // ADA_COL_TILE,),
        in_specs=[
            pl.BlockSpec((rows, d), lambda j: (0, 0)),
            pl.BlockSpec((d, ADA_COL_TILE), lambda j: (0, j)),
            pl.BlockSpec((1, ADA_COL_TILE), lambda j: (0, j)),
        ],
        out_specs=pl.BlockSpec((rows, ADA_COL_TILE), lambda j: (0, j)),
        out_shape=jax.ShapeDtypeStruct((rows, n), F32),
        compiler_params=_params("arbitrary"),
        name="ada",
    )(cc, w_ada, b_ada)


class _Rows:
    def __init__(self, tile, n_prompt_rows, n_latent_rows, dec_seq):
        assert n_prompt_rows % tile == 0 and n_latent_rows % tile == 0 and dec_seq % tile == 0
        self.tile = tile
        self.n_prompt_tiles = n_prompt_rows // tile
        self.n_tiles = (n_prompt_rows + n_latent_rows) // tile
        self.tiles_per_latent_batch = dec_seq // tile

    def prompt_tile(self, i):
        return jnp.minimum(i, self.n_prompt_tiles - 1)

    def latent_tile(self, i):
        return jnp.maximum(i - self.n_prompt_tiles, 0)

    def mod_row(self, i):
        return jnp.where(i < self.n_prompt_tiles, 0,
                         1 + (i - self.n_prompt_tiles) // self.tiles_per_latent_batch)

    def when_prompt(self, i, fn, extra=True):
        pl.when(jnp.logical_and(extra, i < self.n_prompt_tiles))(fn)

    def when_latent(self, i, fn, extra=True):
        pl.when(jnp.logical_and(extra, i >= self.n_prompt_tiles))(fn)


def _inproj_kernel(xp_ref, xs_ref, mod_ref, w_ref, z_ref, h_scr, *, d, rows):
    i = pl.program_id(0)
    first = pl.program_id(1) == 0

    def fill(x_ref):
        shift = mod_ref[0, :, 0:d]
        scale = mod_ref[0, :, d:2 * d]
        for r in range(0, x_ref.shape[0], LN_ROWS):
            rs = slice(r, r + LN_ROWS)
            h_scr[rs, :] = (_ln(x_ref[rs, :]) * (1.0 + scale) + shift).astype(BF16)

    rows.when_prompt(i, lambda: fill(xp_ref), first)
    rows.when_latent(i, lambda: fill(xs_ref), first)
    z_ref[...] = _dot(h_scr[...], w_ref[...].astype(BF16)).astype(BF16)


def _inproj(xp, xs, mod3, w_in, rows):
    d = xp.shape[1]
    n = w_in.shape[1]
    tm = rows.tile
    return pl.pallas_call(
        functools.partial(_inproj_kernel, d=d, rows=rows),
        grid=(rows.n_tiles, n // IN_COL_TILE),
        in_specs=[
            pl.BlockSpec((tm, d), lambda i, j: (rows.prompt_tile(i), 0)),
            pl.BlockSpec((tm, d), lambda i, j: (rows.latent_tile(i), 0)),
            pl.BlockSpec((1, 1, mod3.shape[2]), lambda i, j: (rows.mod_row(i), 0, 0)),
            pl.BlockSpec((d, IN_COL_TILE), lambda i, j: (0, j)),
        ],
        out_specs=pl.BlockSpec((tm, IN_COL_TILE), lambda i, j: (i, j)),
        out_shape=jax.ShapeDtypeStruct((rows.n_tiles * tm, n), BF16),
        scratch_shapes=[pltpu.VMEM((tm, d), BF16)],
        compiler_params=_params("arbitrary", "arbitrary"),
        name="inproj",
    )(xp, xs, mod3, w_in)


def _rope_tables(seq):
    dh = CHUNK
    nf = dh // 4
    pos = np.arange(seq)
    row = (pos // GRID_W).astype(np.float64)
    col = (pos % GRID_W).astype(np.float64)
    freqs = ROPE_THETA ** (-np.arange(nf, dtype=np.float64) / nf)
    ar = row[:, None] * freqs[None, :]
    ac = col[:, None] * freqs[None, :]
    cos = np.concatenate([np.cos(ar), np.cos(ar), np.cos(ac), np.cos(ac)], axis=1)
    sin = np.concatenate([-np.sin(ar), np.sin(ar), -np.sin(ac), np.sin(ac)], axis=1)
    return jnp.asarray(cos, F32), jnp.asarray(sin, F32)


def _retention_kernel(*refs, seq, latent, k_scale):
    if latent:
        (lg_ref, q_ref, k_ref, v_ref, g_ref, gnw_ref, cos_ref, sin_ref, s0_ref,
         o_ref, q_scr, k_scr, o_scr) = refs
        st_ref = None
    else:
        (lg_ref, q_ref, k_ref, v_ref, g_ref, gnw_ref,
         o_ref, st_ref, q_scr, k_scr, o_scr) = refs
        s0_ref = cos_ref = sin_ref = None
    c = CHUNK
    n = seq // c
    h = pl.program_id(1)
    lgf = lg_ref[0, h]
    lgb = lg_ref[1, h]

    ii = lax.broadcasted_iota(I32, (c, c), 0).astype(F32)
    jj = lax.broadcasted_iota(I32, (c, c), 1).astype(F32)
    diff = ii - jj
    neg_inf = jnp.float32(-jnp.inf)
    dmask = (jnp.exp(jnp.where(diff >= 0, diff * lgf, neg_inf))
             + jnp.exp(jnp.where(diff <= 0, (-diff) * lgb, neg_inf)))
    idx = lax.broadcasted_iota(I32, (c, 1), 0).astype(F32)
    qdec_f = jnp.exp((idx + 1.0) * lgf)
    kdec_f = jnp.exp((c - 1.0 - idx) * lgf)
    qdec_b = jnp.exp((c - idx) * lgb)
    kdec_b = jnp.exp(idx * lgb)
    ones_row = jnp.ones((1, c), F32)
    cdec_f = jnp.exp(ones_row * (c * lgf))
    cdec_b = jnp.exp(ones_row * (c * lgb))

    if latent:
        lane = lax.broadcasted_iota(I32, (c, c), 1)
        first_quarter = (lane % (c // 2)) < (c // 4)

        def rope(x, rows):
            partner = jnp.where(first_quarter, pltpu.roll(x, c - c // 4, 1), pltpu.roll(x, c // 4, 1))
            return x * cos_ref[rows, :] + partner * sin_ref[rows, :]

    for ci in range(n):
        rows = slice(ci * c, (ci + 1) * c)
        q = q_ref[rows, :].astype(F32)
        k = k_ref[rows, :].astype(F32) * k_scale
        if latent:
            q = rope(q, rows)
            k = rope(k, rows)
        q_scr[rows, :] = q
        k_scr[rows, :] = k
        scores = _dot_nt(q.astype(BF16), k.astype(BF16)) * dmask
        o_scr[rows, :] = _dot(scores.astype(BF16), v_ref[rows, :])

    s = s0_ref[0] if latent else jnp.zeros((c, c), F32)
    for ci in range(n):
        rows = slice(ci * c, (ci + 1) * c)
        q = q_scr[rows, :]
        k = k_scr[rows, :]
        o_scr[rows, :] += _dot((q * qdec_f).astype(BF16), s.astype(BF16))
        s = s * cdec_f + _dot_tn((k * kdec_f).astype(BF16), v_ref[rows, :])
    if st_ref is not None:
        st_ref[0] = s

    s = s0_ref[1] if latent else jnp.zeros((c, c), F32)
    gnw = gnw_ref[...]
    for ci in reversed(range(n)):
        rows = slice(ci * c, (ci + 1) * c)
        q = q_scr[rows, :]
        k = k_scr[rows, :]
        o = o_scr[rows, :] + _dot((q * qdec_b).astype(BF16), s.astype(BF16))
        s = s * cdec_b + _dot_tn((k * kdec_b).astype(BF16), v_ref[rows, :])
        mu = jnp.mean(o, axis=-1, keepdims=True)
        oc = o - mu
        var = jnp.mean(oc * oc, axis=-1, keepdims=True)
        on = oc * lax.rsqrt(var + GN_EPS) * gnw
        o_ref[rows, :] = (_silu(g_ref[rows, :].astype(F32)) * on).astype(BF16)
    if st_ref is not None:
        st_ref[1] = s


def _retention(z, log_g, gn_w, *, batch, seq, row_block_offset, latent, state=None):
    dh = CHUNK
    d_ret = H_RET * dh
    zspec = lambda off: pl.BlockSpec((seq, dh), lambda b, h: (row_block_offset + b, off + h))
    in_specs = [
        pl.BlockSpec(memory_space=pltpu.SMEM),
        zspec(0), zspec(H_RET), zspec(2 * H_RET), zspec(3 * H_RET),
        pl.BlockSpec((1, dh), lambda b, h: (0, h)),
    ]
    args = [log_g, z, z, z, z, gn_w]
    out_specs = [pl.BlockSpec((seq, dh), lambda b, h: (b, h))]
    out_shape = [jax.ShapeDtypeStruct((batch * seq, d_ret), BF16)]
    state_spec = pl.BlockSpec((None, 2, None, dh, dh), lambda b, h: (b, 0, h, 0, 0))
    if latent:
        cos, sin = _rope_tables(seq)
        in_specs += [pl.BlockSpec((seq, dh), lambda b, h: (0, 0)),
                     pl.BlockSpec((seq, dh), lambda b, h: (0, 0)),
                     state_spec]
        args += [cos, sin, state]
    else:
        out_specs.append(state_spec)
        out_shape.append(jax.ShapeDtypeStruct((batch, 2, H_RET, dh, dh), F32))
    return pl.pallas_call(
        functools.partial(_retention_kernel, seq=seq, latent=latent, k_scale=dh ** -0.5),
        grid=(batch, H_RET),
        in_specs=in_specs,
        out_specs=out_specs,
        out_shape=out_shape,
        scratch_shapes=[pltpu.VMEM((seq, dh), F32), pltpu.VMEM((seq, dh), F32), pltpu.VMEM((seq, dh), F32)],
        compiler_params=_params("arbitrary", "arbitrary"),
        name="retention_latent" if latent else "retention_context",
    )(*args)


def _dft_tables(n):
    jk = np.outer(np.arange(n), np.arange(n)) % n
    ang = 2.0 * np.pi * jk / n
    return np.cos(ang) / np.sqrt(n), np.sin(ang) / np.sqrt(n)


def _fourier_kernel(x_ref, cc_ref, sc_ref, cl_ref, sl_ref, o_ref):
    x = x_ref[...]
    a = _dot(x, cc_ref[...]).astype(BF16)
    b = _dot(x, sc_ref[...]).astype(BF16)
    o_ref[...] = (_dot(cl_ref[...], a) - _dot(sl_ref[...], b)).astype(BF16)


def _fourier(z, *, batch, seq, row_block_offset, col_block_offset, fg):
    cl, sl = _dft_tables(seq)
    cc, sc = _dft_tables(fg)
    tables = [jnp.asarray(t, BF16) for t in (cc, sc, cl, sl)]
    const = lambda n: pl.BlockSpec((n, n), lambda b, g: (0, 0))
    return pl.pallas_call(
        _fourier_kernel,
        grid=(batch, N_FGROUPS),
        in_specs=[pl.BlockSpec((seq, fg), lambda b, g: (row_block_offset + b, col_block_offset + g)),
                  const(fg), const(fg), const(seq), const(seq)],
        out_specs=pl.BlockSpec((seq, fg), lambda b, g: (b, g)),
        out_shape=jax.ShapeDtypeStruct((batch * seq, N_FGROUPS * fg), BF16),
        compiler_params=_params("arbitrary", "arbitrary"),
        name=f"fourier_{seq}",
    )(z, *tables)


def _outproj_kernel(retp_ref, rets_ref, fourp_ref, fours_ref, w_ref, xp_ref, xs_ref, mod_ref,
                    lnw_ref, lnb_ref, wr_ref, x1_ref, hu_ref, lg_ref, *, d, d_ret, alpha, rows):
    i = pl.program_id(0)

    def body(ret_ref, four_ref, x_ref):
        gate1 = mod_ref[0, :, 2 * d:3 * d]
        shift2 = mod_ref[0, :, 3 * d:4 * d]
        scale2 = mod_ref[0, :, 4 * d:5 * d]
        mix = _dot(ret_ref[...], w_ref[0:d_ret, :]) + _dot(four_ref[...], w_ref[d_ret:, :])
        x1 = _ln(alpha * x_ref[...] + gate1 * mix) * lnw_ref[...] + lnb_ref[...]
        x1_ref[...] = x1
        h2 = _ln(x1) * (1.0 + scale2) + shift2
        hu_ref[...] = _pack_halves(h2)
        wr = wr_ref[...]
        w_hi = wr.astype(BF16)
        w_lo = (wr - w_hi.astype(F32)).astype(BF16)
        h_hi = h2.astype(BF16)
        h_lo = (h2 - h_hi.astype(F32)).astype(BF16)
        lg_ref[...] = _dot_nt(w_hi, h_hi) + (_dot_nt(w_hi, h_lo) + _dot_nt(w_lo, h_hi))

    rows.when_prompt(i, lambda: body(retp_ref, fourp_ref, xp_ref))
    rows.when_latent(i, lambda: body(rets_ref, fours_ref, xs_ref))


def _outproj(retp, rets, fourp, fours, w_out_bf16, xp, xs, mod3, ln_w, ln_b, w_router_t, rows, alpha):
    d = xp.shape[1]
    d_ret = retp.shape[1]
    d_four = fourp.shape[1]
    tm = rows.tile
    t = rows.n_tiles * tm
    n_e = w_router_t.shape[0]
    prompt = lambda w: pl.BlockSpec((tm, w), lambda i: (rows.prompt_tile(i), 0))
    latent = lambda w: pl.BlockSpec((tm, w), lambda i: (rows.latent_tile(i), 0))
    const = lambda a: pl.BlockSpec(a.shape, lambda i: (0, 0))
    return pl.pallas_call(
        functools.partial(_outproj_kernel, d=d, d_ret=d_ret, alpha=alpha, rows=rows),
        grid=(rows.n_tiles,),
        in_specs=[
            prompt(d_ret), latent(d_ret), prompt(d_four), latent(d_four),
            const(w_out_bf16),
            prompt(d), latent(d),
            pl.BlockSpec((1, 1, mod3.shape[2]), lambda i: (rows.mod_row(i), 0, 0)),
            const(ln_w), const(ln_b), const(w_router_t),
        ],
        out_specs=[
            pl.BlockSpec((tm, d), lambda i: (i, 0)),
            pl.BlockSpec((tm, d // 2), lambda i: (i, 0)),
            pl.BlockSpec((n_e, tm), lambda i: (0, i)),
        ],
        out_shape=[
            jax.ShapeDtypeStruct((t, d), F32),
            jax.ShapeDtypeStruct((t, d // 2), U32),
            jax.ShapeDtypeStruct((n_e, t), F32),
        ],
        compiler_params=_params("arbitrary"),
        name="outproj",
    )(retp, rets, fourp, fours, w_out_bf16, xp, xs, mod3, ln_w, ln_b, w_router_t)


def _first_index_of_max(vals, index, sentinel, axis):
    m = jnp.max(vals, axis=axis, keepdims=True)
    return jnp.min(jnp.where(vals == m, index, sentinel), axis=axis, keepdims=True), m


def _router_kernel(lg_ref, bias_ref, eidx_ref, w_ref, rank_ref, cnt_ref, carry):
    i = pl.program_id(0)
    tr = lg_ref.shape[1]
    gsz = N_EXPERTS // N_GROUPS
    neg_inf = jnp.float32(-jnp.inf)

    @pl.when(i == 0)
    def _():
        carry[...] = jnp.zeros_like(carry)

    scores = jax.nn.sigmoid(lg_ref[...])
    biased = scores + bias_ref[...]
    b3 = biased.reshape(N_GROUPS, gsz, tr)
    sub = lax.broadcasted_iota(I32, (N_GROUPS, gsz, tr), 1)
    first, m1 = _first_index_of_max(b3, sub, gsz, 1)
    m2 = jnp.max(jnp.where(sub == first, neg_inf, b3), axis=1, keepdims=True)
    gscore = (m1 + m2).reshape(N_GROUPS, tr)

    gi = lax.broadcasted_iota(I32, (N_GROUPS, tr), 0)
    gsel = jnp.zeros((N_GROUPS, tr), jnp.bool_)
    cur = gscore
    for _ in range(TOPK_GROUP):
        first, _m = _first_index_of_max(cur, gi, N_GROUPS, 0)
        pick = gi == first
        gsel = jnp.logical_or(gsel, pick)
        cur = jnp.where(pick, neg_inf, cur)

    gsel3 = jnp.broadcast_to(gsel.reshape(N_GROUPS, 1, tr), (N_GROUPS, gsz, tr))
    cur = jnp.where(gsel3, b3, neg_inf).reshape(N_EXPERTS, tr)
    ei = lax.broadcasted_iota(I32, (N_EXPERTS, tr), 0)
    picks = []
    sel_scores = []
    for _ in range(TOP_K):
        first, _m = _first_index_of_max(cur, ei, N_EXPERTS, 0)
        pick = ei == first
        picks.append(first)
        sel_scores.append(jnp.sum(jnp.where(pick, scores, 0.0), axis=0, keepdims=True))
        cur = jnp.where(pick, neg_inf, cur)
    total = sel_scores[0]
    for sc in sel_scores[1:]:
        total = total + sc

    member = jnp.zeros((N_EXPERTS, tr), F32)
    for first in picks:
        member = member + jnp.where(ei == first, 1.0, 0.0)
    src = lax.broadcasted_iota(I32, (tr, tr), 0)
    dst = lax.broadcasted_iota(I32, (tr, tr), 1)
    upper = jnp.where(src < dst, 1.0, 0.0).astype(BF16)
    rank_all = _dot(member.astype(BF16), upper) + carry[...]
    for k in range(TOP_K):
        pick = ei == picks[k]
        eidx_ref[k:k + 1, :] = picks[k]
        w_ref[k:k + 1, :] = sel_scores[k] / total * ROUTED_SCALE
        rank_ref[k:k + 1, :] = jnp.sum(jnp.where(pick, rank_all, 0.0), axis=0, keepdims=True).astype(I32)
    carry[...] = carry[...] + jnp.sum(member, axis=1, keepdims=True)
    cnt_ref[...] = carry[...]


def _router(logits_t, bias):
    n_e, t = logits_t.shape
    tr = ROUTER_TILE
    tok = pl.BlockSpec((TOP_K, tr), lambda i: (0, i))
    return pl.pallas_call(
        _router_kernel,
        grid=(t // tr,),
        in_specs=[pl.BlockSpec((n_e, tr), lambda i: (0, i)),
                  pl.BlockSpec((n_e, 1), lambda i: (0, 0))],
        out_specs=[tok, tok, tok, pl.BlockSpec((n_e, 1), lambda i: (0, 0))],
        out_shape=[jax.ShapeDtypeStruct((TOP_K, t), I32),
                   jax.ShapeDtypeStruct((TOP_K, t), F32),
                   jax.ShapeDtypeStruct((TOP_K, t), I32),
                   jax.ShapeDtypeStruct((n_e, 1), F32)],
        scratch_shapes=[pltpu.VMEM((n_e, 1), F32)],
        compiler_params=_params("arbitrary"),
        name="router",
    )(logits_t, bias)


def _positions_kernel(start_ref, eidx_ref, rank_ref, pos_ref):
    e = eidx_ref[...]
    pos = rank_ref[...]
    for x in range(N_EXPERTS):
        pos = pos + jnp.where(e == x, start_ref[x], 0)
    pos_ref[...] = pos


def _positions(start, eidx, rank):
    full = pl.BlockSpec(eidx.shape, lambda i: (0, 0))
    return pl.pallas_call(
        _positions_kernel,
        grid=(1,),
        in_specs=[pl.BlockSpec(memory_space=pltpu.SMEM), full, full],
        out_specs=full,
        out_shape=jax.ShapeDtypeStruct(eidx.shape, I32),
        compiler_params=_params("arbitrary"),
        name="positions",
    )(start, eidx, rank)


def _row_copy(src_ref, src_row, dst_ref, dst_row, sem):
    return pltpu.make_async_copy(src_ref.at[pl.ds(src_row, 1)], dst_ref.at[pl.ds(dst_row, 1)], sem)


def _dispatch_kernel(pos_ref, h_ref, xs_ref, sem):
    td = h_ref.shape[0]

    def issue(t, carry):
        for k in range(TOP_K):
            _row_copy(h_ref, t, xs_ref, pos_ref[k, t], sem).start()
        return carry

    lax.fori_loop(0, td, issue, 0)

    def drain(t, carry):
        for k in range(TOP_K):
            _row_copy(h_ref, 0, xs_ref, 0, sem).wait()
        return carry

    lax.fori_loop(0, td, drain, 0)


def _dispatch(pos, hu):
    t, w = hu.shape
    td = DISPATCH_TILE
    return pl.pallas_call(
        _dispatch_kernel,
        grid=(t // td,),
        in_specs=[pl.BlockSpec((TOP_K, td), lambda i: (0, i), memory_space=pltpu.SMEM),
                  pl.BlockSpec((td, w), lambda i: (i, 0))],
        out_specs=pl.BlockSpec(memory_space=pl.ANY),
        out_shape=jax.ShapeDtypeStruct((t * TOP_K, w), hu.dtype),
        scratch_shapes=[pltpu.SemaphoreType.DMA],
        compiler_params=_params("arbitrary"),
        name="dispatch",
    )(pos, hu)


def _experts_kernel(blk_ref, exp_ref, lo_ref, hi_ref, np_ref, x_ref, wg_ref, wu_ref, wd_ref,
                    y_ref, wg_s, wu_s, wd_s):
    s = pl.program_id(0)
    prev = jnp.maximum(s - 1, 0)
    valid = s < np_ref[0]
    new_expert = jnp.logical_or(s == 0, exp_ref[s] != exp_ref[prev])
    first_visit = jnp.logical_or(s == 0, blk_ref[s] != blk_ref[prev])
    tm = x_ref.shape[0]

    @pl.when(jnp.logical_and(valid, new_expert))
    def _():
        wg_s[...] = wg_ref[0].astype(BF16)
        wu_s[...] = wu_ref[0].astype(BF16)
        wd_s[...] = wd_ref[0].astype(BF16)

    def compute():
        lo, hi = _unpack_halves(x_ref[...])
        lo = lo.astype(BF16)
        hi = hi.astype(BF16)
        a = (_silu(_dot_halves(lo, hi, wg_s)) * _dot_halves(lo, hi, wu_s)).astype(BF16)
        y = _pack_halves(_dot(a, wd_s[...]))
        rows = blk_ref[s] * tm + lax.broadcasted_iota(I32, (tm, 1), 0)
        mine = jnp.logical_and(rows >= lo_ref[s], rows < hi_ref[s])
        return y, mine

    @pl.when(jnp.logical_and(valid, first_visit))
    def _():
        y, mine = compute()
        y_ref[...] = jnp.where(mine, y, jnp.uint32(0))

    @pl.when(jnp.logical_and(valid, jnp.logical_not(first_visit)))
    def _():
        y, mine = compute()
        y_ref[...] = jnp.where(mine, y, y_ref[...])


def _experts(meta, xs, w_gate_e, w_up_e, w_down_e):
    blk, exp, lo, hi, n_pairs = meta
    n_rows, w = xs.shape
    d = w_gate_e.shape[1]
    d_e = w_gate_e.shape[2]
    tm = EXPERT_ROWS
    n_steps = blk.shape[0]
    grid_spec = pltpu.PrefetchScalarGridSpec(
        num_scalar_prefetch=5,
        grid=(n_steps,),
        in_specs=[
            pl.BlockSpec((tm, w), lambda s, blk, exp, lo, hi, n: (blk[s], 0)),
            pl.BlockSpec((1, d, d_e), lambda s, blk, exp, lo, hi, n: (exp[s], 0, 0)),
            pl.BlockSpec((1, d, d_e), lambda s, blk, exp, lo, hi, n: (exp[s], 0, 0)),
            pl.BlockSpec((1, d_e, d), lambda s, blk, exp, lo, hi, n: (exp[s], 0, 0)),
        ],
        out_specs=pl.BlockSpec((tm, w), lambda s, blk, exp, lo, hi, n: (blk[s], 0)),
        scratch_shapes=[pltpu.VMEM((d, d_e), BF16), pltpu.VMEM((d, d_e), BF16), pltpu.VMEM((d_e, d), BF16)],
    )
    return pl.pallas_call(
        _experts_kernel,
        grid_spec=grid_spec,
        out_shape=jax.ShapeDtypeStruct((n_rows, w), xs.dtype),
        compiler_params=_params("arbitrary"),
        name="experts",
    )(blk, exp, lo, hi, n_pairs, xs, w_gate_e, w_up_e, w_down_e)


def _lookup(table, idx):
    hit = idx[:, None] == jnp.arange(table.shape[0], dtype=I32)[None, :]
    return jnp.sum(jnp.where(hit, table[None, :], 0), axis=1)


def _expert_schedule(counts, n_rows):
    tm = EXPERT_ROWS
    n_blocks = n_rows // tm
    n_steps = n_blocks + N_EXPERTS - 1
    end = jnp.cumsum(counts)
    start = end - counts
    first_blk = start // tm
    last_blk = (end - 1) // tm
    n_blk = jnp.where(counts > 0, last_blk - first_blk + 1, 0)
    pair_end = jnp.cumsum(n_blk)
    pair_start = pair_end - n_blk
    n_pairs = pair_end[-1]
    step = jnp.minimum(jnp.arange(n_steps, dtype=I32), n_pairs - 1)
    exp = jnp.sum((pair_end[None, :] <= step[:, None]).astype(I32), axis=1)
    blk = _lookup(first_blk - pair_start, exp) + step
    return (blk, exp, _lookup(start, exp), _lookup(end, exp), n_pairs.reshape(1)), start


def _cast_kernel(x_ref, o_ref):
    o_ref[...] = x_ref[...].astype(o_ref.dtype)


def _cast_bf16(w):
    r, c = w.shape
    tile = min(r, 512)
    return pl.pallas_call(
        _cast_kernel,
        grid=(r // tile,),
        in_specs=[pl.BlockSpec((tile, c), lambda i: (i, 0))],
        out_specs=pl.BlockSpec((tile, c), lambda i: (i, 0)),
        out_shape=jax.ShapeDtypeStruct((r, c), BF16),
        compiler_params=_params("arbitrary"),
        name="cast_bf16",
    )(w)


def _combine_kernel(pos_ref, w_ref, hu_ref, x1_ref, mod_ref, lnw_ref, lnb_ref, wg_ref, wu_ref, wd_ref,
                    y_ref, op_ref, os_ref, gbuf, sem, *, d, alpha, rows):
    i = pl.program_id(0)
    tc = hu_ref.shape[0]

    def issue(t, carry):
        for k in range(TOP_K):
            _row_copy(y_ref, pos_ref[k, t], gbuf.at[k], t, sem).start()
        return carry

    lax.fori_loop(0, tc, issue, 0)

    lo, hi = _unpack_halves(hu_ref[...])
    lo = lo.astype(BF16)
    hi = hi.astype(BF16)
    a = (_silu(_dot_halves(lo, hi, wg_ref)) * _dot_halves(lo, hi, wu_ref)).astype(BF16)
    shared = _dot(a, wd_ref[...])

    def drain(t, carry):
        for k in range(TOP_K):
            _row_copy(y_ref, 0, gbuf.at[k], 0, sem).wait()
        return carry

    lax.fori_loop(0, tc, drain, 0)

    w = w_ref[...]
    r_lo = jnp.zeros((tc, d // 2), F32)
    r_hi = jnp.zeros((tc, d // 2), F32)
    for k in range(TOP_K):
        lo, hi = _unpack_halves(gbuf[k])
        r_lo = r_lo + lo * w[:, k:k + 1]
        r_hi = r_hi + hi * w[:, k:k + 1]
    m = shared + jnp.concatenate([r_lo, r_hi], axis=1)
    gate2 = mod_ref[0, :, 5 * d:6 * d]
    out = _ln(alpha * x1_ref[...] + gate2 * m) * lnw_ref[...] + lnb_ref[...]

    def store(o_ref):
        o_ref[...] = out

    rows.when_prompt(i, lambda: store(op_ref))
    rows.when_latent(i, lambda: store(os_ref))


def _combine(pos, wts, hu, x1, mod3, ln_w, ln_b, wg, wu, wd, ybuf, rows, n_prompt_rows, alpha):
    t, d = x1.shape
    tc = rows.tile
    const = lambda a: pl.BlockSpec(a.shape, lambda i: (0, 0))
    return pl.pallas_call(
        functools.partial(_combine_kernel, d=d, alpha=alpha, rows=rows),
        grid=(rows.n_tiles,),
        in_specs=[
            pl.BlockSpec((TOP_K, tc), lambda i: (0, i), memory_space=pltpu.SMEM),
            pl.BlockSpec((tc, TOP_K), lambda i: (i, 0)),
            pl.BlockSpec((tc, d // 2), lambda i: (i, 0)),
            pl.BlockSpec((tc, d), lambda i: (i, 0)),
            pl.BlockSpec((1, 1, mod3.shape[2]), lambda i: (rows.mod_row(i), 0, 0)),
            const(ln_w), const(ln_b), const(wg), const(wu), const(wd),
            pl.BlockSpec(memory_space=pl.ANY),
        ],
        out_specs=[pl.BlockSpec((tc, d), lambda i: (rows.prompt_tile(i), 0)),
                   pl.BlockSpec((tc, d), lambda i: (rows.latent_tile(i), 0))],
        out_shape=[jax.ShapeDtypeStruct((n_prompt_rows, d), F32),
                   jax.ShapeDtypeStruct((t - n_prompt_rows, d), F32)],
        scratch_shapes=[pltpu.VMEM((TOP_K, tc, d // 2), U32), pltpu.SemaphoreType.DMA],
        compiler_params=_params("arbitrary"),
        name="combine",
    )(pos, wts, hu, x1, mod3, ln_w, ln_b, wg, wu, wd, ybuf)


def kernel(x_prompt, x_sample, c, state_ret, c_ctx, w_ada, b_ada, w_in, ret_decay, ret_gn_w, w_out,
           ln1_w, ln1_b, w_router, router_bias, w_gate_e, w_up_e, w_down_e, w_gate_s, w_up_s, w_down_s,
           ln2_w, ln2_b):
    batch, seq, d = x_prompt.shape
    dec_batch, dec_seq, _ = x_sample.shape
    depth = w_in.shape[0]
    assert depth == 1
    n_prompt_rows = batch * seq
    n_latent_rows = dec_batch * dec_seq
    t = n_prompt_rows + n_latent_rows
    alpha = (2.0 * depth) ** 0.25
    d_ret = H_RET * CHUNK
    fg = (w_in.shape[2] - 4 * d_ret) // N_FGROUPS
    l = 0
    tiling = lambda tile: _Rows(tile, n_prompt_rows, n_latent_rows, dec_seq)

    xp = x_prompt.reshape(n_prompt_rows, d)
    xs = x_sample.reshape(n_latent_rows, d)

    n_mod = 1 + dec_batch
    cc = jnp.concatenate([c_ctx[None, :], c, jnp.zeros((-n_mod % 8, d), F32)], axis=0)
    mod = _ada(cc, w_ada[l], b_ada[l][None, :])
    mod3 = mod.reshape(mod.shape[0], 1, mod.shape[1])

    z = _inproj(xp, xs, mod3, w_in[l], tiling(ROW_TILE))

    log_g = jax.nn.log_sigmoid(ret_decay[l].astype(F32))
    gn_w = ret_gn_w[l][None, :]
    latent_row_block = n_prompt_rows // dec_seq
    ret_p, ctx_state = _retention(z, log_g, gn_w, batch=batch, seq=seq, row_block_offset=0, latent=False)
    ret_s = _retention(z, log_g, gn_w, batch=dec_batch, seq=dec_seq, row_block_offset=latent_row_block,
                       latent=True, state=state_ret[:, l])[0]
    fcol = 4 * d_ret // fg
    four_p = _fourier(z, batch=batch, seq=seq, row_block_offset=0, col_block_offset=fcol, fg=fg)
    four_s = _fourier(z, batch=dec_batch, seq=dec_seq, row_block_offset=latent_row_block,
                      col_block_offset=fcol, fg=fg)

    x1, hu, logits_t = _outproj(ret_p, ret_s, four_p, four_s, _cast_bf16(w_out[l]), xp, xs, mod3,
                                ln1_w[l][None, :], ln1_b[l][None, :], w_router[l].T,
                                tiling(OUT_ROW_TILE), alpha)

    eidx, wts, rank, counts = _router(logits_t, router_bias[l][:, None])
    meta, start = _expert_schedule(counts[:, 0].astype(I32), t * TOP_K)
    pos = _positions(start, eidx, rank)

    xsorted = _dispatch(pos, hu)
    ybuf = _experts(meta, xsorted, w_gate_e[l], w_up_e[l], w_down_e[l])
    y_prompt, y_sample = _combine(pos, wts.T, hu, x1, mod3, ln2_w[l][None, :], ln2_b[l][None, :],
                                  _cast_bf16(w_gate_s[l]), _cast_bf16(w_up_s[l]), _cast_bf16(w_down_s[l]),
                                  ybuf, tiling(COMBINE_TILE), n_prompt_rows, alpha)

    new_state = ctx_state[:, None].astype(x_prompt.dtype)
    return (y_prompt.reshape(batch, seq, d), y_sample.reshape(dec_batch, dec_seq, d), new_state)
```

```python
import functools

import numpy as np
import jax
import jax.numpy as jnp
from jax import lax
from jax.experimental import pallas as pl
from jax.experimental.pallas import tpu as pltpu

F32 = jnp.float32
BF16 = jnp.bfloat16
I32 = jnp.int32
U32 = jnp.uint32

GRID_W = 64
H_RET = 8
N_FGROUPS = 4
CHUNK = 128
ROPE_THETA = 10000.0
N_EXPERTS = 64
TOP_K = 8
N_GROUPS = 8
TOPK_GROUP = 4
ROUTED_SCALE = 2.5
LN_EPS = 1e-6
GN_EPS = 1e-5
N_DMA_PRIORITIES = 2

VMEM_LIMIT_BYTES = 56 * 1024 * 1024

ROW_TILE = 1024
IN_COL_TILE = 512
OUT_ROW_TILE = 256
ADA_COL_TILE = 1024
ROUTER_TILE = 512
DISPATCH_TILE = 256
EXPERT_ROWS = 256
COMBINE_TILE = 256
CHUNK_ROWS = 16
CHUNKS_PER_ITER = 4


def _params(*sem):
    return pltpu.CompilerParams(dimension_semantics=sem, vmem_limit_bytes=VMEM_LIMIT_BYTES)


def _ln(x):
    mu = jnp.mean(x, axis=-1, keepdims=True)
    xc = x - mu
    var = jnp.mean(xc * xc, axis=-1, keepdims=True)
    return xc * lax.rsqrt(var + LN_EPS)


def _silu(x):
    return x * jax.nn.sigmoid(x)


def _dot(a, b):
    return jnp.dot(a, b, preferred_element_type=F32)


def _dot_nt(a, b):
    return lax.dot_general(a, b, (((1,), (1,)), ((), ())), preferred_element_type=F32)


def _dot_tn(a, b):
    return lax.dot_general(a, b, (((0,), (0,)), ((), ())), preferred_element_type=F32)


def _pack_halves(x):
    n = x.shape[1] // 2
    lo = lax.bitcast_convert_type(x[:, :n].astype(BF16).astype(F32), U32)
    hi = lax.bitcast_convert_type(x[:, n:].astype(BF16).astype(F32), U32)
    return (lo >> 16) | hi


def _unpack_halves(u):
    lo = lax.bitcast_convert_type(u << 16, F32)
    hi = lax.bitcast_convert_type(u & jnp.uint32(0xFFFF0000), F32)
    return lo, hi


def _dot_halves(lo, hi, w_ref):
    n = lo.shape[1]
    return _dot(lo, w_ref[0:n, :]) + _dot(hi, w_ref[n:2 * n, :])


def _for_row_chunks(n_rows, fn):
    per_iter = CHUNK_ROWS * CHUNKS_PER_ITER
    assert n_rows % per_iter == 0

    def body(it, carry):
        for c in range(CHUNKS_PER_ITER):
            start = pl.multiple_of(it * per_iter + c * CHUNK_ROWS, CHUNK_ROWS)
            fn(pl.ds(start, CHUNK_ROWS))
        return carry

    lax.fori_loop(0, n_rows // per_iter, body, 0)


def _ada_kernel(c_ref, w_ref, b_ref, o_ref):
    s = _silu(c_ref[...]).astype(BF16)
    o_ref[...] = _dot(s, w_ref[...].astype(BF16)) + b_ref[...]


def _ada(cc, w_ada, b_ada):
    rows, d = cc.shape
    n = w_ada.shape[1]
    return pl.pallas_call(
        _ada_kernel,
        grid=(n // ADA_COL_TILE,),
        in_specs=[
            pl.BlockSpec((rows, d), lambda j: (0, 0)),
            pl.BlockSpec((d, ADA_COL_TILE), lambda j: (0, j)),
            pl.BlockSpec((1, ADA_COL_TILE), lambda j: (0, j)),
        ],
        out_specs=pl.BlockSpec((rows, ADA_COL_TILE), lambda j: (0, j)),
        out_shape=jax.ShapeDtypeStruct((rows, n), F32),
        compiler_params=_params("arbitrary"),
        name="ada",
    )(cc, w_ada, b_ada)


class _Rows:
    def __init__(self, tile, n_prompt_rows, n_latent_rows, dec_seq):
        assert n_prompt_rows % tile == 0 and n_latent_rows % tile == 0 and dec_seq % tile == 0
        self.tile = tile
        self.n_prompt_tiles = n_prompt_rows // tile
        self.n_tiles = (n_prompt_rows + n_latent_rows) // tile
        self.tiles_per_latent_batch = dec_seq // tile

    def prompt_tile(self, i):
        return jnp.minimum(i, self.n_prompt_tiles - 1)

    def latent_tile(self, i):
        return jnp.maximum(i - self.n_prompt_tiles, 0)

    def mod_row(self, i):
        return jnp.where(i < self.n_prompt_tiles, 0,
                         1 + (i - self.n_prompt_tiles) // self.tiles_per_latent_batch)

    def when_prompt(self, i, fn, extra=True):
        pl.when(jnp.logical_and(extra, i < self.n_prompt_tiles))(fn)

    def when_latent(self, i, fn, extra=True):
        pl.when(jnp.logical_and(extra, i >= self.n_prompt_tiles))(fn)


def _inproj_kernel(xp_ref, xs_ref, mod_ref, w_ref, z_ref, h_scr, *, d, rows):
    i = pl.program_id(0)
    first = pl.program_id(1) == 0

    def fill(x_ref):
        shift = mod_ref[0, :, 0:d]
        scale = mod_ref[0, :, d:2 * d]

        def chunk(rs):
            h_scr[rs, :] = (_ln(x_ref[rs, :]) * (1.0 + scale) + shift).astype(BF16)

        _for_row_chunks(x_ref.shape[0], chunk)

    rows.when_prompt(i, lambda: fill(xp_ref), first)
    rows.when_latent(i, lambda: fill(xs_ref), first)
    z_ref[...] = _dot(h_scr[...], w_ref[...]).astype(BF16)


def _inproj(xp, xs, mod3, w_in, rows):
    d = xp.shape[1]
    n = w_in.shape[1]
    tm = rows.tile
    return pl.pallas_call(
        functools.partial(_inproj_kernel, d=d, rows=rows),
        grid=(rows.n_tiles, n // IN_COL_TILE),
        in_specs=[
            pl.BlockSpec((tm, d), lambda i, j: (rows.prompt_tile(i), 0)),
            pl.BlockSpec((tm, d), lambda i, j: (rows.latent_tile(i), 0)),
            pl.BlockSpec((1, 1, mod3.shape[2]), lambda i, j: (rows.mod_row(i), 0, 0)),
            pl.BlockSpec((d, IN_COL_TILE), lambda i, j: (0, j)),
        ],
        out_specs=pl.BlockSpec((tm, IN_COL_TILE), lambda i, j: (i, j)),
        out_shape=jax.ShapeDtypeStruct((rows.n_tiles * tm, n), BF16),
        scratch_shapes=[pltpu.VMEM((tm, d), BF16)],
        compiler_params=_params("arbitrary", "arbitrary"),
        name="inproj",
    )(xp, xs, mod3, w_in)


def _rope_tables(seq):
    dh = CHUNK
    nf = dh // 4
    pos = np.arange(seq)
    row = (pos // GRID_W).astype(np.float64)
    col = (pos % GRID_W).astype(np.float64)
    freqs = ROPE_THETA ** (-np.arange(nf, dtype=np.float64) / nf)
    ar = row[:, None] * freqs[None, :]
    ac = col[:, None] * freqs[None, :]
    cos = np.concatenate([np.cos(ar), np.cos(ar), np.cos(ac), np.cos(ac)], axis=1)
    sin = np.concatenate([-np.sin(ar), np.sin(ar), -np.sin(ac), np.sin(ac)], axis=1)
    return jnp.asarray(cos, F32), jnp.asarray(sin, F32)


def _retention_kernel(*refs, seq, latent, k_scale):
    if latent:
        (lg_ref, q_ref, k_ref, v_ref, g_ref, gnw_ref, cos_ref, sin_ref, s0_ref,
         o_ref, q_scr, k_scr, o_scr) = refs
        st_ref = None
    else:
        (lg_ref, q_ref, k_ref, v_ref, g_ref, gnw_ref,
         o_ref, st_ref, q_scr, k_scr, o_scr) = refs
        s0_ref = cos_ref = sin_ref = None
    c = CHUNK
    n = seq // c
    h = pl.program_id(1)
    lgf = lg_ref[0, h]
    lgb = lg_ref[1, h]

    ii = lax.broadcasted_iota(I32, (c, c), 0).astype(F32)
    jj = lax.broadcasted_iota(I32, (c, c), 1).astype(F32)
    diff = ii - jj
    neg_inf = jnp.float32(-jnp.inf)
    dmask = (jnp.exp(jnp.where(diff >= 0, diff * lgf, neg_inf))
             + jnp.exp(jnp.where(diff <= 0, (-diff) * lgb, neg_inf)))
    idx = lax.broadcasted_iota(I32, (c, 1), 0).astype(F32)
    qdec_f = jnp.exp((idx + 1.0) * lgf)
    kdec_f = jnp.exp((c - 1.0 - idx) * lgf)
    qdec_b = jnp.exp((c - idx) * lgb)
    kdec_b = jnp.exp(idx * lgb)
    ones_row = jnp.ones((1, c), F32)
    cdec_f = jnp.exp(ones_row * (c * lgf))
    cdec_b = jnp.exp(ones_row * (c * lgb))

    if latent:
        lane = lax.broadcasted_iota(I32, (c, c), 1)
        first_quarter = (lane % (c // 2)) < (c // 4)

        def rope(x, rows):
            partner = jnp.where(first_quarter, pltpu.roll(x, c - c // 4, 1), pltpu.roll(x, c // 4, 1))
            return x * cos_ref[rows, :] + partner * sin_ref[rows, :]

    for ci in range(n):
        rows = slice(ci * c, (ci + 1) * c)
        q = q_ref[rows, :].astype(F32)
        k = k_ref[rows, :].astype(F32) * k_scale
        if latent:
            q = rope(q, rows)
            k = rope(k, rows)
        q_scr[rows, :] = q
        k_scr[rows, :] = k
        scores = _dot_nt(q.astype(BF16), k.astype(BF16)) * dmask
        o_scr[rows, :] = _dot(scores.astype(BF16), v_ref[rows, :])

    s = s0_ref[0] if latent else jnp.zeros((c, c), F32)
    for ci in range(n):
        rows = slice(ci * c, (ci + 1) * c)
        q = q_scr[rows, :]
        k = k_scr[rows, :]
        o_scr[rows, :] += _dot((q * qdec_f).astype(BF16), s.astype(BF16))
        s = s * cdec_f + _dot_tn((k * kdec_f).astype(BF16), v_ref[rows, :])
    if st_ref is not None:
        st_ref[0] = s

    s = s0_ref[1] if latent else jnp.zeros((c, c), F32)
    gnw = gnw_ref[...]
    for ci in reversed(range(n)):
        rows = slice(ci * c, (ci + 1) * c)
        q = q_scr[rows, :]
        k = k_scr[rows, :]
        o = o_scr[rows, :] + _dot((q * qdec_b).astype(BF16), s.astype(BF16))
        s = s * cdec_b + _dot_tn((k * kdec_b).astype(BF16), v_ref[rows, :])
        mu = jnp.mean(o, axis=-1, keepdims=True)
        oc = o - mu
        var = jnp.mean(oc * oc, axis=-1, keepdims=True)
        on = oc * lax.rsqrt(var + GN_EPS) * gnw
        o_ref[rows, :] = (_silu(g_ref[rows, :].astype(F32)) * on).astype(BF16)
    if st_ref is not None:
        st_ref[1] = s


def _retention(z, log_g, gn_w, *, batch, seq, row_block_offset, latent, state=None):
    dh = CHUNK
    d_ret = H_RET * dh
    zspec = lambda off: pl.BlockSpec((seq, dh), lambda b, h: (row_block_offset + b, off + h))
    in_specs = [
        pl.BlockSpec(memory_space=pltpu.SMEM),
        zspec(0), zspec(H_RET), zspec(2 * H_RET), zspec(3 * H_RET),
        pl.BlockSpec((1, dh), lambda b, h: (0, h)),
    ]
    args = [log_g, z, z, z, z, gn_w]
    out_specs = [pl.BlockSpec((seq, dh), lambda b, h: (b, h))]
    out_shape = [jax.ShapeDtypeStruct((batch * seq, d_ret), BF16)]
    state_spec = pl.BlockSpec((None, 2, None, dh, dh), lambda b, h: (b, 0, h, 0, 0))
    if latent:
        cos, sin = _rope_tables(seq)
        in_specs += [pl.BlockSpec((seq, dh), lambda b, h: (0, 0)),
                     pl.BlockSpec((seq, dh), lambda b, h: (0, 0)),
                     state_spec]
        args += [cos, sin, state]
    else:
        out_specs.append(state_spec)
        out_shape.append(jax.ShapeDtypeStruct((batch, 2, H_RET, dh, dh), F32))
    return pl.pallas_call(
        functools.partial(_retention_kernel, seq=seq, latent=latent, k_scale=dh ** -0.5),
        grid=(batch, H_RET),
        in_specs=in_specs,
        out_specs=out_specs,
        out_shape=out_shape,
        scratch_shapes=[pltpu.VMEM((seq, dh), F32), pltpu.VMEM((seq, dh), F32), pltpu.VMEM((seq, dh), F32)],
        compiler_params=_params("arbitrary", "arbitrary"),
        name="retention_latent" if latent else "retention_context",
    )(*args)


def _dft_tables(n):
    jk = np.outer(np.arange(n), np.arange(n)) % n
    ang = 2.0 * np.pi * jk / n
    return np.cos(ang) / np.sqrt(n), np.sin(ang) / np.sqrt(n)


def _fourier_kernel(x_ref, cc_ref, sc_ref, cl_ref, sl_ref, o_ref):
    x = x_ref[...]
    a = _dot(x, cc_ref[...]).astype(BF16)
    b = _dot(x, sc_ref[...]).astype(BF16)
    o_ref[...] = (_dot(cl_ref[...], a) - _dot(sl_ref[...], b)).astype(BF16)


def _fourier(z, *, batch, seq, row_block_offset, col_block_offset, fg):
    cl, sl = _dft_tables(seq)
    cc, sc = _dft_tables(fg)
    tables = [jnp.asarray(t, BF16) for t in (cc, sc, cl, sl)]
    const = lambda n: pl.BlockSpec((n, n), lambda b, g: (0, 0))
    return pl.pallas_call(
        _fourier_kernel,
        grid=(batch, N_FGROUPS),
        in_specs=[pl.BlockSpec((seq, fg), lambda b, g: (row_block_offset + b, col_block_offset + g)),
                  const(fg), const(fg), const(seq), const(seq)],
        out_specs=pl.BlockSpec((seq, fg), lambda b, g: (b, g)),
        out_shape=jax.ShapeDtypeStruct((batch * seq, N_FGROUPS * fg), BF16),
        compiler_params=_params("arbitrary", "arbitrary"),
        name=f"fourier_{seq}",
    )(z, *tables)


def _outproj_kernel(retp_ref, rets_ref, fourp_ref, fours_ref, w_ref, xp_ref, xs_ref, mod_ref,
                    lnw_ref, lnb_ref, wr_ref, x1_ref, hu_ref, lg_ref, mix_scr, hhi_scr, hlo_scr,
                    *, d, d_ret, alpha, rows):
    i = pl.program_id(0)

    def body(ret_ref, four_ref, x_ref):
        mix_scr[...] = _dot(ret_ref[...], w_ref[0:d_ret, :]) + _dot(four_ref[...], w_ref[d_ret:, :])

        def chunk(rs):
            gate1 = mod_ref[0, :, 2 * d:3 * d]
            shift2 = mod_ref[0, :, 3 * d:4 * d]
            scale2 = mod_ref[0, :, 4 * d:5 * d]
            x1 = _ln(alpha * x_ref[rs, :] + gate1 * mix_scr[rs, :]) * lnw_ref[...] + lnb_ref[...]
            x1_ref[rs, :] = x1
            h2 = _ln(x1) * (1.0 + scale2) + shift2
            hu_ref[rs, :] = _pack_halves(h2)
            h_hi = h2.astype(BF16)
            hhi_scr[rs, :] = h_hi
            hlo_scr[rs, :] = (h2 - h_hi.astype(F32)).astype(BF16)

        _for_row_chunks(x_ref.shape[0], chunk)
        wr = wr_ref[...]
        w_hi = wr.astype(BF16)
        w_lo = (wr - w_hi.astype(F32)).astype(BF16)
        lg_ref[...] = (_dot_nt(w_hi, hhi_scr[...])
                       + (_dot_nt(w_hi, hlo_scr[...]) + _dot_nt(w_lo, hhi_scr[...])))

    rows.when_prompt(i, lambda: body(retp_ref, fourp_ref, xp_ref))
    rows.when_latent(i, lambda: body(rets_ref, fours_ref, xs_ref))


def _outproj(retp, rets, fourp, fours, w_out_bf16, xp, xs, mod3, ln_w, ln_b, w_router_t, rows, alpha):
    d = xp.shape[1]
    d_ret = retp.shape[1]
    d_four = fourp.shape[1]
    tm = rows.tile
    t = rows.n_tiles * tm
    n_e = w_router_t.shape[0]
    prompt = lambda w: pl.BlockSpec((tm, w), lambda i: (rows.prompt_tile(i), 0))
    latent = lambda w: pl.BlockSpec((tm, w), lambda i: (rows.latent_tile(i), 0))
    const = lambda a: pl.BlockSpec(a.shape, lambda i: (0, 0))
    return pl.pallas_call(
        functools.partial(_outproj_kernel, d=d, d_ret=d_ret, alpha=alpha, rows=rows),
        grid=(rows.n_tiles,),
        in_specs=[
            prompt(d_ret), latent(d_ret), prompt(d_four), latent(d_four),
            const(w_out_bf16),
            prompt(d), latent(d),
            pl.BlockSpec((1, 1, mod3.shape[2]), lambda i: (rows.mod_row(i), 0, 0)),
            const(ln_w), const(ln_b), const(w_router_t),
        ],
        out_specs=[
            pl.BlockSpec((tm, d), lambda i: (i, 0)),
            pl.BlockSpec((tm, d // 2), lambda i: (i, 0)),
            pl.BlockSpec((n_e, tm), lambda i: (0, i)),
        ],
        out_shape=[
            jax.ShapeDtypeStruct((t, d), F32),
            jax.ShapeDtypeStruct((t, d // 2), U32),
            jax.ShapeDtypeStruct((n_e, t), F32),
        ],
        scratch_shapes=[pltpu.VMEM((tm, d), F32), pltpu.VMEM((tm, d), BF16), pltpu.VMEM((tm, d), BF16)],
        compiler_params=_params("arbitrary"),
        name="outproj",
    )(retp, rets, fourp, fours, w_out_bf16, xp, xs, mod3, ln_w, ln_b, w_router_t)


def _first_index_of_max(vals, index, sentinel, axis):
    m = jnp.max(vals, axis=axis, keepdims=True)
    return jnp.min(jnp.where(vals == m, index, sentinel), axis=axis, keepdims=True), m


def _router_kernel(lg_ref, bias_ref, eidx_ref, w_ref, rank_ref, cnt_ref, carry):
    i = pl.program_id(0)
    tr = lg_ref.shape[1]
    gsz = N_EXPERTS // N_GROUPS
    neg_inf = jnp.float32(-jnp.inf)

    @pl.when(i == 0)
    def _():
        carry[...] = jnp.zeros_like(carry)

    scores = jax.nn.sigmoid(lg_ref[...])
    biased = scores + bias_ref[...]
    b3 = biased.reshape(N_GROUPS, gsz, tr)
    sub = lax.broadcasted_iota(I32, (N_GROUPS, gsz, tr), 1)
    first, m1 = _first_index_of_max(b3, sub, gsz, 1)
    m2 = jnp.max(jnp.where(sub == first, neg_inf, b3), axis=1, keepdims=True)
    gscore = (m1 + m2).reshape(N_GROUPS, tr)

    gi = lax.broadcasted_iota(I32, (N_GROUPS, tr), 0)
    gsel = jnp.zeros((N_GROUPS, tr), jnp.bool_)
    cur = gscore
    for _ in range(TOPK_GROUP):
        first, _m = _first_index_of_max(cur, gi, N_GROUPS, 0)
        pick = gi == first
        gsel = jnp.logical_or(gsel, pick)
        cur = jnp.where(pick, neg_inf, cur)

    gsel3 = jnp.broadcast_to(gsel.reshape(N_GROUPS, 1, tr), (N_GROUPS, gsz, tr))
    cur = jnp.where(gsel3, b3, neg_inf).reshape(N_EXPERTS, tr)
    ei = lax.broadcasted_iota(I32, (N_EXPERTS, tr), 0)
    picks = []
    sel_scores = []
    for _ in range(TOP_K):
        first, _m = _first_index_of_max(cur, ei, N_EXPERTS, 0)
        pick = ei == first
        picks.append(first)
        sel_scores.append(jnp.sum(jnp.where(pick, scores, 0.0), axis=0, keepdims=True))
        cur = jnp.where(pick, neg_inf, cur)
    total = sel_scores[0]
    for sc in sel_scores[1:]:
        total = total + sc

    member = jnp.zeros((N_EXPERTS, tr), F32)
    for first in picks:
        member = member + jnp.where(ei == first, 1.0, 0.0)
    src = lax.broadcasted_iota(I32, (tr, tr), 0)
    dst = lax.broadcasted_iota(I32, (tr, tr), 1)
    upper = jnp.where(src < dst, 1.0, 0.0).astype(BF16)
    rank_all = _dot(member.astype(BF16), upper) + carry[...]
    for k in range(TOP_K):
        pick = ei == picks[k]
        eidx_ref[k:k + 1, :] = picks[k]
        w_ref[k:k + 1, :] = sel_scores[k] / total * ROUTED_SCALE
        rank_ref[k:k + 1, :] = jnp.sum(jnp.where(pick, rank_all, 0.0), axis=0, keepdims=True).astype(I32)
    carry[...] = carry[...] + jnp.sum(member, axis=1, keepdims=True)
    cnt_ref[...] = carry[...]


def _router(logits_t, bias):
    n_e, t = logits_t.shape
    tr = ROUTER_TILE
    tok = pl.BlockSpec((TOP_K, tr), lambda i: (0, i))
    return pl.pallas_call(
        _router_kernel,
        grid=(t // tr,),
        in_specs=[pl.BlockSpec((n_e, tr), lambda i: (0, i)),
                  pl.BlockSpec((n_e, 1), lambda i: (0, 0))],
        out_specs=[tok, tok, tok, pl.BlockSpec((n_e, 1), lambda i: (0, 0))],
        out_shape=[jax.ShapeDtypeStruct((TOP_K, t), I32),
                   jax.ShapeDtypeStruct((TOP_K, t), F32),
                   jax.ShapeDtypeStruct((TOP_K, t), I32),
                   jax.ShapeDtypeStruct((n_e, 1), F32)],
        scratch_shapes=[pltpu.VMEM((n_e, 1), F32)],
        compiler_params=_params("arbitrary"),
        name="router",
    )(logits_t, bias)


def _positions_kernel(start_ref, eidx_ref, rank_ref, pos_ref):
    e = eidx_ref[...]
    pos = rank_ref[...]
    for x in range(N_EXPERTS):
        pos = pos + jnp.where(e == x, start_ref[x], 0)
    pos_ref[...] = pos


def _positions(start, eidx, rank):
    full = pl.BlockSpec(eidx.shape, lambda i: (0, 0))
    return pl.pallas_call(
        _positions_kernel,
        grid=(1,),
        in_specs=[pl.BlockSpec(memory_space=pltpu.SMEM), full, full],
        out_specs=full,
        out_shape=jax.ShapeDtypeStruct(eidx.shape, I32),
        compiler_params=_params("arbitrary"),
        name="positions",
    )(start, eidx, rank)


def _row_copy(src_ref, src_row, dst_ref, dst_row, sem):
    return pltpu.make_async_copy(src_ref.at[pl.ds(src_row, 1)], dst_ref.at[pl.ds(dst_row, 1)], sem)


def _dispatch_kernel(pos_ref, h_ref, xs_ref, sem):
    td = h_ref.shape[0]

    def issue(t, carry):
        for k in range(TOP_K):
            _row_copy(h_ref, t, xs_ref, pos_ref[k, t], sem).start(priority=k % N_DMA_PRIORITIES)
        return carry

    lax.fori_loop(0, td, issue, 0)

    def drain(t, carry):
        for k in range(TOP_K):
            _row_copy(h_ref, 0, xs_ref, 0, sem).wait()
        return carry

    lax.fori_loop(0, td, drain, 0)


def _dispatch(pos, hu):
    t, w = hu.shape
    td = DISPATCH_TILE
    return pl.pallas_call(
        _dispatch_kernel,
        grid=(t // td,),
        in_specs=[pl.BlockSpec((TOP_K, td), lambda i: (0, i), memory_space=pltpu.SMEM),
                  pl.BlockSpec((td, w), lambda i: (i, 0))],
        out_specs=pl.BlockSpec(memory_space=pl.ANY),
        out_shape=jax.ShapeDtypeStruct((t * TOP_K, w), hu.dtype),
        scratch_shapes=[pltpu.SemaphoreType.DMA],
        compiler_params=_params("arbitrary"),
        name="dispatch",
    )(pos, hu)


def _experts_kernel(blk_ref, exp_ref, lo_ref, hi_ref, np_ref, slot_ref, nxt_ref, x_ref,
                    wg_hbm, wu_hbm, wd_hbm, y_ref, wg_f, wu_f, wd_f, wg_s, wu_s, wd_s, sems):
    s = pl.program_id(0)
    prev = jnp.maximum(s - 1, 0)
    valid = s < np_ref[0]
    new_expert = jnp.logical_or(s == 0, exp_ref[s] != exp_ref[prev])
    first_visit = jnp.logical_or(s == 0, blk_ref[s] != blk_ref[prev])
    tm = x_ref.shape[0]

    def weight_copies(e, slot):
        return [pltpu.make_async_copy(hbm.at[e], buf.at[slot], sems.at[slot, m])
                for m, (hbm, buf) in enumerate(((wg_hbm, wg_f), (wu_hbm, wu_f), (wd_hbm, wd_f)))]

    @pl.when(s == 0)
    def _():
        for cp in weight_copies(exp_ref[0], slot_ref[0]):
            cp.start()

    @pl.when(jnp.logical_and(valid, new_expert))
    def _():
        slot = slot_ref[s]
        for cp in weight_copies(exp_ref[s], slot):
            cp.wait()

        @pl.when(nxt_ref[s] >= 0)
        def _():
            for cp in weight_copies(nxt_ref[s], 1 - slot):
                cp.start()

        wg_s[...] = wg_f[slot].astype(BF16)
        wu_s[...] = wu_f[slot].astype(BF16)
        wd_s[...] = wd_f[slot].astype(BF16)

    def compute():
        lo, hi = _unpack_halves(x_ref[...])
        lo = lo.astype(BF16)
        hi = hi.astype(BF16)
        a = (_silu(_dot_halves(lo, hi, wg_s)) * _dot_halves(lo, hi, wu_s)).astype(BF16)
        y = _pack_halves(_dot(a, wd_s[...]))
        rows = blk_ref[s] * tm + lax.broadcasted_iota(I32, (tm, 1), 0)
        mine = jnp.logical_and(rows >= lo_ref[s], rows < hi_ref[s])
        return y, mine

    @pl.when(jnp.logical_and(valid, first_visit))
    def _():
        y, mine = compute()
        y_ref[...] = jnp.where(mine, y, jnp.uint32(0))

    @pl.when(jnp.logical_and(valid, jnp.logical_not(first_visit)))
    def _():
        y, mine = compute()
        y_ref[...] = jnp.where(mine, y, y_ref[...])


def _experts(meta, xs, w_gate_e, w_up_e, w_down_e):
    n_rows, w = xs.shape
    d = w_gate_e.shape[1]
    d_e = w_gate_e.shape[2]
    tm = EXPERT_ROWS
    n_steps = meta[0].shape[0]
    rows_of_step = lambda s, blk, *_: (blk[s], 0)
    hbm = pl.BlockSpec(memory_space=pl.ANY)
    grid_spec = pltpu.PrefetchScalarGridSpec(
        num_scalar_prefetch=len(meta),
        grid=(n_steps,),
        in_specs=[pl.BlockSpec((tm, w), rows_of_step), hbm, hbm, hbm],
        out_specs=pl.BlockSpec((tm, w), rows_of_step),
        scratch_shapes=[pltpu.VMEM((2, d, d_e), F32), pltpu.VMEM((2, d, d_e), F32), pltpu.VMEM((2, d_e, d), F32),
                        pltpu.VMEM((d, d_e), BF16), pltpu.VMEM((d, d_e), BF16), pltpu.VMEM((d_e, d), BF16),
                        pltpu.SemaphoreType.DMA((2, 3))],
    )
    return pl.pallas_call(
        _experts_kernel,
        grid_spec=grid_spec,
        out_shape=jax.ShapeDtypeStruct((n_rows, w), xs.dtype),
        compiler_params=_params("arbitrary"),
        name="experts",
    )(*meta, xs, w_gate_e, w_up_e, w_down_e)


def _lookup(table, idx):
    hit = idx[:, None] == jnp.arange(table.shape[0], dtype=I32)[None, :]
    return jnp.sum(jnp.where(hit, table[None, :], 0), axis=1)


def _expert_schedule(counts, n_rows):
    tm = EXPERT_ROWS
    n_blocks = n_rows // tm
    n_steps = n_blocks + N_EXPERTS - 1
    end = jnp.cumsum(counts)
    start = end - counts
    first_blk = start // tm
    last_blk = (end - 1) // tm
    n_blk = jnp.where(counts > 0, last_blk - first_blk + 1, 0)
    pair_end = jnp.cumsum(n_blk)
    pair_start = pair_end - n_blk
    n_pairs = pair_end[-1]
    step = jnp.minimum(jnp.arange(n_steps, dtype=I32), n_pairs - 1)
    exp = jnp.sum((pair_end[None, :] <= step[:, None]).astype(I32), axis=1)
    blk = _lookup(first_blk - pair_start, exp) + step
    active = n_blk > 0
    order = jnp.cumsum(active.astype(I32)) - 1
    ids = jnp.arange(N_EXPERTS, dtype=I32)
    later = jnp.where(jnp.logical_and(active[None, :], ids[None, :] > ids[:, None]), ids[None, :], N_EXPERTS)
    following = jnp.min(later, axis=1)
    following = jnp.where(following == N_EXPERTS, -1, following)
    meta = (blk, exp, _lookup(start, exp), _lookup(end, exp), n_pairs.reshape(1),
            _lookup(order % 2, exp), _lookup(following, exp))
    return meta, start


def _cast_kernel(x_ref, o_ref):
    o_ref[...] = x_ref[...].astype(o_ref.dtype)


def _cast_bf16(w):
    r, c = w.shape
    tile = min(r, 512)
    return pl.pallas_call(
        _cast_kernel,
        grid=(r // tile,),
        in_specs=[pl.BlockSpec((tile, c), lambda i: (i, 0))],
        out_specs=pl.BlockSpec((tile, c), lambda i: (i, 0)),
        out_shape=jax.ShapeDtypeStruct((r, c), BF16),
        compiler_params=_params("arbitrary"),
        name="cast_bf16",
    )(w)


def _combine_kernel(pos_ref, w_ref, hu_ref, x1_ref, mod_ref, lnw_ref, lnb_ref, wg_ref, wu_ref, wd_ref,
                    y_ref, op_ref, os_ref, gbuf, shared_scr, sem, *, d, alpha, rows):
    i = pl.program_id(0)
    tc = hu_ref.shape[0]

    def issue(t, carry):
        for k in range(TOP_K):
            _row_copy(y_ref, pos_ref[k, t], gbuf.at[k], t, sem).start(priority=k % N_DMA_PRIORITIES)
        return carry

    lax.fori_loop(0, tc, issue, 0)

    lo, hi = _unpack_halves(hu_ref[...])
    lo = lo.astype(BF16)
    hi = hi.astype(BF16)
    a = (_silu(_dot_halves(lo, hi, wg_ref)) * _dot_halves(lo, hi, wu_ref)).astype(BF16)
    shared_scr[...] = _dot(a, wd_ref[...])

    def drain(t, carry):
        for k in range(TOP_K):
            _row_copy(y_ref, 0, gbuf.at[k], 0, sem).wait()
        return carry

    lax.fori_loop(0, tc, drain, 0)

    def finish(o_ref):
        def chunk(rs):
            w = w_ref[rs, :]
            r_lo = jnp.zeros((CHUNK_ROWS, d // 2), F32)
            r_hi = jnp.zeros((CHUNK_ROWS, d // 2), F32)
            for k in range(TOP_K):
                lo, hi = _unpack_halves(gbuf[k, rs, :])
                r_lo = r_lo + lo * w[:, k:k + 1]
                r_hi = r_hi + hi * w[:, k:k + 1]
            m = shared_scr[rs, :] + jnp.concatenate([r_lo, r_hi], axis=1)
            gate2 = mod_ref[0, :, 5 * d:6 * d]
            o_ref[rs, :] = _ln(alpha * x1_ref[rs, :] + gate2 * m) * lnw_ref[...] + lnb_ref[...]

        _for_row_chunks(tc, chunk)

    rows.when_prompt(i, lambda: finish(op_ref))
    rows.when_latent(i, lambda: finish(os_ref))


def _combine(pos, wts, hu, x1, mod3, ln_w, ln_b, wg, wu, wd, ybuf, rows, n_prompt_rows, alpha):
    t, d = x1.shape
    tc = rows.tile
    const = lambda a: pl.BlockSpec(a.shape, lambda i: (0, 0))
    return pl.pallas_call(
        functools.partial(_combine_kernel, d=d, alpha=alpha, rows=rows),
        grid=(rows.n_tiles,),
        in_specs=[
            pl.BlockSpec((TOP_K, tc), lambda i: (0, i), memory_space=pltpu.SMEM),
            pl.BlockSpec((tc, TOP_K), lambda i: (i, 0)),
            pl.BlockSpec((tc, d // 2), lambda i: (i, 0)),
            pl.BlockSpec((tc, d), lambda i: (i, 0)),
            pl.BlockSpec((1, 1, mod3.shape[2]), lambda i: (rows.mod_row(i), 0, 0)),
            const(ln_w), const(ln_b), const(wg), const(wu), const(wd),
            pl.BlockSpec(memory_space=pl.ANY),
        ],
        out_specs=[pl.BlockSpec((tc, d), lambda i: (rows.prompt_tile(i), 0)),
                   pl.BlockSpec((tc, d), lambda i: (rows.latent_tile(i), 0))],
        out_shape=[jax.ShapeDtypeStruct((n_prompt_rows, d), F32),
                   jax.ShapeDtypeStruct((t - n_prompt_rows, d), F32)],
        scratch_shapes=[pltpu.VMEM((TOP_K, tc, d // 2), U32), pltpu.VMEM((tc, d), F32),
                        pltpu.SemaphoreType.DMA],
        compiler_params=_params("arbitrary"),
        name="combine",
    )(pos, wts, hu, x1, mod3, ln_w, ln_b, wg, wu, wd, ybuf)


def kernel(x_prompt, x_sample, c, state_ret, c_ctx, w_ada, b_ada, w_in, ret_decay, ret_gn_w, w_out,
           ln1_w, ln1_b, w_router, router_bias, w_gate_e, w_up_e, w_down_e, w_gate_s, w_up_s, w_down_s,
           ln2_w, ln2_b):
    batch, seq, d = x_prompt.shape
    dec_batch, dec_seq, _ = x_sample.shape
    depth = w_in.shape[0]
    assert depth == 1
    n_prompt_rows = batch * seq
    n_latent_rows = dec_batch * dec_seq
    t = n_prompt_rows + n_latent_rows
    alpha = (2.0 * depth) ** 0.25
    d_ret = H_RET * CHUNK
    fg = (w_in.shape[2] - 4 * d_ret) // N_FGROUPS
    l = 0
    tiling = lambda tile: _Rows(tile, n_prompt_rows, n_latent_rows, dec_seq)

    xp = x_prompt.reshape(n_prompt_rows, d)
    xs = x_sample.reshape(n_latent_rows, d)

    n_mod = 1 + dec_batch
    cc = jnp.concatenate([c_ctx[None, :], c, jnp.zeros((-n_mod % 8, d), F32)], axis=0)
    mod = _ada(cc, w_ada[l], b_ada[l][None, :])
    mod3 = mod.reshape(mod.shape[0], 1, mod.shape[1])

    z = _inproj(xp, xs, mod3, _cast_bf16(w_in[l]), tiling(ROW_TILE))

    log_g = jax.nn.log_sigmoid(ret_decay[l].astype(F32))
    gn_w = ret_gn_w[l][None, :]
    latent_row_block = n_prompt_rows // dec_seq
    ret_p, ctx_state = _retention(z, log_g, gn_w, batch=batch, seq=seq, row_block_offset=0, latent=False)
    ret_s = _retention(z, log_g, gn_w, batch=dec_batch, seq=dec_seq, row_block_offset=latent_row_block,
                       latent=True, state=state_ret[:, l])[0]
    fcol = 4 * d_ret // fg
    four_p = _fourier(z, batch=batch, seq=seq, row_block_offset=0, col_block_offset=fcol, fg=fg)
    four_s = _fourier(z, batch=dec_batch, seq=dec_seq, row_block_offset=latent_row_block,
                      col_block_offset=fcol, fg=fg)

    x1, hu, logits_t = _outproj(ret_p, ret_s, four_p, four_s, _cast_bf16(w_out[l]), xp, xs, mod3,
                                ln1_w[l][None, :], ln1_b[l][None, :], w_router[l].T,
                                tiling(OUT_ROW_TILE), alpha)

    eidx, wts, rank, counts = _router(logits_t, router_bias[l][:, None])
    meta, start = _expert_schedule(counts[:, 0].astype(I32), t * TOP_K)
    pos = _positions(start, eidx, rank)

    xsorted = _dispatch(pos, hu)
    ybuf = _experts(meta, xsorted, w_gate_e[l], w_up_e[l], w_down_e[l])
    y_prompt, y_sample = _combine(pos, wts.T, hu, x1, mod3, ln2_w[l][None, :], ln2_b[l][None, :],
                                  _cast_bf16(w_gate_s[l]), _cast_bf16(w_up_s[l]), _cast_bf16(w_down_s[l]),
                                  ybuf, tiling(COMBINE_TILE), n_prompt_rows, alpha)

    new_state = ctx_state[:, None].astype(x_prompt.dtype)
    return (y_prompt.reshape(batch, seq, d), y_sample.reshape(dec_batch, dec_seq, d), new_state)
```

```python
import functools

import numpy as np
import jax
import jax.numpy as jnp
from jax import lax
from jax.experimental import pallas as pl
from jax.experimental.pallas import tpu as pltpu

F32 = jnp.float32
BF16 = jnp.bfloat16
I32 = jnp.int32
U32 = jnp.uint32

GRID_W = 64
H_RET = 8
N_FGROUPS = 4
CHUNK = 128
ROPE_THETA = 10000.0
N_EXPERTS = 64
TOP_K = 8
N_GROUPS = 8
TOPK_GROUP = 4
ROUTED_SCALE = 2.5
LN_EPS = 1e-6
GN_EPS = 1e-5
N_DMA_PRIORITIES = 2

VMEM_LIMIT_BYTES = 56 * 1024 * 1024

ROW_TILE = 1024
IN_COL_TILE = 512
OUT_ROW_TILE = 256
ADA_COL_TILE = 1024
ROUTER_TILE = 512
DISPATCH_TILE = 256
EXPERT_ROWS = 256
COMBINE_TILE = 256
LN_ROWS = 256
RET_HEADS_PER_STEP = 4


def _params(*sem):
    return pltpu.CompilerParams(dimension_semantics=sem, vmem_limit_bytes=VMEM_LIMIT_BYTES)


def _ln(x):
    mu = jnp.mean(x, axis=-1, keepdims=True)
    xc = x - mu
    var = jnp.mean(xc * xc, axis=-1, keepdims=True)
    return xc * lax.rsqrt(var + LN_EPS)


def _silu(x):
    return x * jax.nn.sigmoid(x)


def _dot(a, b):
    return jnp.dot(a, b, preferred_element_type=F32)


def _dot_nt(a, b):
    return lax.dot_general(a, b, (((1,), (1,)), ((), ())), preferred_element_type=F32)


def _dot_tn(a, b):
    return lax.dot_general(a, b, (((0,), (0,)), ((), ())), preferred_element_type=F32)


def _pack_halves(x):
    n = x.shape[1] // 2
    lo = lax.bitcast_convert_type(x[:, :n].astype(BF16).astype(F32), U32)
    hi = lax.bitcast_convert_type(x[:, n:].astype(BF16).astype(F32), U32)
    return (lo >> 16) | hi


def _unpack_halves(u):
    lo = lax.bitcast_convert_type(u << 16, F32)
    hi = lax.bitcast_convert_type(u & jnp.uint32(0xFFFF0000), F32)
    return lo, hi


def _dot_halves(lo, hi, w_ref):
    n = lo.shape[1]
    return _dot(lo, w_ref[0:n, :]) + _dot(hi, w_ref[n:2 * n, :])


def _ada_kernel(c_ref, w_ref, b_ref, o_ref):
    s = _silu(c_ref[...]).astype(BF16)
    o_ref[...] = _dot(s, w_ref[...].astype(BF16)) + b_ref[...]


def _ada(cc, w_ada, b_ada):
    rows, d = cc.shape
    n = w_ada.shape[1]
    return pl.pallas_call(
        _ada_kernel,
        grid=(n // ADA_COL_TILE,),
        in_specs=[
            pl.BlockSpec((rows, d), lambda j: (0, 0)),
            pl.BlockSpec((d, ADA_COL_TILE), lambda j: (0, j)),
            pl.BlockSpec((1, ADA_COL_TILE), lambda j: (0, j)),
        ],
        out_specs=pl.BlockSpec((rows, ADA_COL_TILE), lambda j: (0, j)),
        out_shape=jax.ShapeDtypeStruct((rows, n), F32),
        compiler_params=_params("arbitrary"),
        name="ada",
    )(cc, w_ada, b_ada)


class _Rows:
    def __init__(self, tile, n_prompt_rows, n_latent_rows, dec_seq):
        assert n_prompt_rows % tile == 0 and n_latent_rows % tile == 0 and dec_seq % tile == 0
        self.tile = tile
        self.n_prompt_tiles = n_prompt_rows // tile
        self.n_tiles = (n_prompt_rows + n_latent_rows) // tile
        self.tiles_per_latent_batch = dec_seq // tile

    def prompt_tile(self, i):
        return jnp.minimum(i, self.n_prompt_tiles - 1)

    def latent_tile(self, i):
        return jnp.maximum(i - self.n_prompt_tiles, 0)

    def mod_row(self, i):
        return jnp.where(i < self.n_prompt_tiles, 0,
                         1 + (i - self.n_prompt_tiles) // self.tiles_per_latent_batch)

    def when_prompt(self, i, fn, extra=True):
        pl.when(jnp.logical_and(extra, i < self.n_prompt_tiles))(fn)

    def when_latent(self, i, fn, extra=True):
        pl.when(jnp.logical_and(extra, i >= self.n_prompt_tiles))(fn)


def _inproj_kernel(xp_ref, xs_ref, mod_ref, w_ref, z_ref, h_scr, *, d, rows):
    i = pl.program_id(0)
    first = pl.program_id(1) == 0

    def fill(x_ref):
        shift = mod_ref[0, :, 0:d]
        scale = mod_ref[0, :, d:2 * d]
        for r in range(0, x_ref.shape[0], LN_ROWS):
            rs = slice(r, r + LN_ROWS)
            h_scr[rs, :] = (_ln(x_ref[rs, :]) * (1.0 + scale) + shift).astype(BF16)

    rows.when_prompt(i, lambda: fill(xp_ref), first)
    rows.when_latent(i, lambda: fill(xs_ref), first)
    z_ref[...] = _dot(h_scr[...], w_ref[...].astype(BF16)).astype(BF16)


def _inproj(xp, xs, mod3, w_in, rows):
    d = xp.shape[1]
    n = w_in.shape[1]
    tm = rows.tile
    return pl.pallas_call(
        functools.partial(_inproj_kernel, d=d, rows=rows),
        grid=(rows.n_tiles, n // IN_COL_TILE),
        in_specs=[
            pl.BlockSpec((tm, d), lambda i, j: (rows.prompt_tile(i), 0)),
            pl.BlockSpec((tm, d), lambda i, j: (rows.latent_tile(i), 0)),
            pl.BlockSpec((1, 1, mod3.shape[2]), lambda i, j: (rows.mod_row(i), 0, 0)),
            pl.BlockSpec((d, IN_COL_TILE), lambda i, j: (0, j)),
        ],
        out_specs=pl.BlockSpec((tm, IN_COL_TILE), lambda i, j: (i, j)),
        out_shape=jax.ShapeDtypeStruct((rows.n_tiles * tm, n), BF16),
        scratch_shapes=[pltpu.VMEM((tm, d), BF16)],
        compiler_params=_params("arbitrary", "arbitrary"),
        name="inproj",
    )(xp, xs, mod3, w_in)


def _rope_tables(seq):
    dh = CHUNK
    nf = dh // 4
    pos = np.arange(seq)
    row = (pos // GRID_W).astype(np.float64)
    col = (pos % GRID_W).astype(np.float64)
    freqs = ROPE_THETA ** (-np.arange(nf, dtype=np.float64) / nf)
    ar = row[:, None] * freqs[None, :]
    ac = col[:, None] * freqs[None, :]
    cos = np.concatenate([np.cos(ar), np.cos(ar), np.cos(ac), np.cos(ac)], axis=1)
    sin = np.concatenate([-np.sin(ar), np.sin(ar), -np.sin(ac), np.sin(ac)], axis=1)
    return jnp.asarray(cos, F32), jnp.asarray(sin, F32)


def _retention_kernel(*refs, seq, latent, k_scale):
    if latent:
        (lg_ref, q_ref, k_ref, v_ref, g_ref, gnw_ref, cos_ref, sin_ref, s0_ref,
         o_ref, q_scr, k_scr, o_scr) = refs
        st_ref = None
    else:
        (lg_ref, q_ref, k_ref, v_ref, g_ref, gnw_ref,
         o_ref, st_ref, q_scr, k_scr, o_scr) = refs
        s0_ref = cos_ref = sin_ref = None
    c = CHUNK
    n = seq // c
    heads = range(RET_HEADS_PER_STEP)
    h0 = pl.program_id(1) * RET_HEADS_PER_STEP
    cols = lambda hh: slice(hh * c, (hh + 1) * c)

    ii = lax.broadcasted_iota(I32, (c, c), 0).astype(F32)
    jj = lax.broadcasted_iota(I32, (c, c), 1).astype(F32)
    diff = ii - jj
    neg_inf = jnp.float32(-jnp.inf)
    idx = lax.broadcasted_iota(I32, (c, 1), 0).astype(F32)
    ones_row = jnp.ones((1, c), F32)

    def decays(hh):
        lgf = lg_ref[0, h0 + hh]
        lgb = lg_ref[1, h0 + hh]
        return dict(
            dmask=(jnp.exp(jnp.where(diff >= 0, diff * lgf, neg_inf))
                   + jnp.exp(jnp.where(diff <= 0, (-diff) * lgb, neg_inf))),
            qdec_f=jnp.exp((idx + 1.0) * lgf), kdec_f=jnp.exp((c - 1.0 - idx) * lgf),
            qdec_b=jnp.exp((c - idx) * lgb), kdec_b=jnp.exp(idx * lgb),
            cdec_f=jnp.exp(ones_row * (c * lgf)), cdec_b=jnp.exp(ones_row * (c * lgb)))

    dec = [decays(hh) for hh in heads]

    if latent:
        lane = lax.broadcasted_iota(I32, (c, c), 1)
        first_quarter = (lane % (c // 2)) < (c // 4)

        def rope(x, rows):
            partner = jnp.where(first_quarter, pltpu.roll(x, c - c // 4, 1), pltpu.roll(x, c // 4, 1))
            return x * cos_ref[rows, :] + partner * sin_ref[rows, :]

    for ci in range(n):
        rows = slice(ci * c, (ci + 1) * c)
        for hh in heads:
            q = q_ref[rows, cols(hh)].astype(F32)
            k = k_ref[rows, cols(hh)].astype(F32) * k_scale
            if latent:
                q = rope(q, rows)
                k = rope(k, rows)
            q_scr[rows, cols(hh)] = q
            k_scr[rows, cols(hh)] = k
            scores = _dot_nt(q.astype(BF16), k.astype(BF16)) * dec[hh]["dmask"]
            o_scr[rows, cols(hh)] = _dot(scores.astype(BF16), v_ref[rows, cols(hh)])

    s = [s0_ref[0, hh] if latent else jnp.zeros((c, c), F32) for hh in heads]
    for ci in range(n):
        rows = slice(ci * c, (ci + 1) * c)
        for hh in heads:
            q = q_scr[rows, cols(hh)]
            k = k_scr[rows, cols(hh)]
            o_scr[rows, cols(hh)] += _dot((q * dec[hh]["qdec_f"]).astype(BF16), s[hh].astype(BF16))
            s[hh] = s[hh] * dec[hh]["cdec_f"] + _dot_tn((k * dec[hh]["kdec_f"]).astype(BF16),
                                                          v_ref[rows, cols(hh)])
    if st_ref is not None:
        for hh in heads:
            st_ref[0, hh] = s[hh]

    s = [s0_ref[1, hh] if latent else jnp.zeros((c, c), F32) for hh in heads]
    for ci in reversed(range(n)):
        rows = slice(ci * c, (ci + 1) * c)
        for hh in heads:
            q = q_scr[rows, cols(hh)]
            k = k_scr[rows, cols(hh)]
            o = o_scr[rows, cols(hh)] + _dot((q * dec[hh]["qdec_b"]).astype(BF16), s[hh].astype(BF16))
            s[hh] = s[hh] * dec[hh]["cdec_b"] + _dot_tn((k * dec[hh]["kdec_b"]).astype(BF16),
                                                          v_ref[rows, cols(hh)])
            mu = jnp.mean(o, axis=-1, keepdims=True)
            oc = o - mu
            var = jnp.mean(oc * oc, axis=-1, keepdims=True)
            on = oc * lax.rsqrt(var + GN_EPS) * gnw_ref[:, cols(hh)]
            o_ref[rows, cols(hh)] = (_silu(g_ref[rows, cols(hh)].astype(F32)) * on).astype(BF16)
    if st_ref is not None:
        for hh in heads:
            st_ref[1, hh] = s[hh]


def _retention(z, log_g, gn_w, *, batch, seq, row_block_offset, latent, state=None):
    dh = CHUNK
    hp = RET_HEADS_PER_STEP
    d_ret = H_RET * dh
    zspec = lambda part: pl.BlockSpec((seq, hp * dh),
                                      lambda b, h: (row_block_offset + b, part * (H_RET // hp) + h))
    in_specs = [
        pl.BlockSpec(memory_space=pltpu.SMEM),
        zspec(0), zspec(1), zspec(2), zspec(3),
        pl.BlockSpec((1, hp * dh), lambda b, h: (0, h)),
    ]
    args = [log_g, z, z, z, z, gn_w]
    out_specs = [pl.BlockSpec((seq, hp * dh), lambda b, h: (b, h))]
    out_shape = [jax.ShapeDtypeStruct((batch * seq, d_ret), BF16)]
    state_spec = pl.BlockSpec((None, 2, hp, dh, dh), lambda b, h: (b, 0, h, 0, 0))
    if latent:
        cos, sin = _rope_tables(seq)
        in_specs += [pl.BlockSpec((seq, dh), lambda b, h: (0, 0)),
                     pl.BlockSpec((seq, dh), lambda b, h: (0, 0)),
                     state_spec]
        args += [cos, sin, state]
    else:
        out_specs.append(state_spec)
        out_shape.append(jax.ShapeDtypeStruct((batch, 2, H_RET, dh, dh), F32))
    return pl.pallas_call(
        functools.partial(_retention_kernel, seq=seq, latent=latent, k_scale=dh ** -0.5),
        grid=(batch, H_RET // hp),
        in_specs=in_specs,
        out_specs=out_specs,
        out_shape=out_shape,
        scratch_shapes=[pltpu.VMEM((seq, hp * dh), F32)] * 3,
        compiler_params=_params("arbitrary", "arbitrary"),
        name="retention_latent" if latent else "retention_context",
    )(*args)


def _dft_tables(n):
    jk = np.outer(np.arange(n), np.arange(n)) % n
    ang = 2.0 * np.pi * jk / n
    return np.cos(ang) / np.sqrt(n), np.sin(ang) / np.sqrt(n)


def _fourier_kernel(x_ref, cc_ref, sc_ref, cl_ref, sl_ref, o_ref):
    x = x_ref[...]
    a = _dot(x, cc_ref[...]).astype(BF16)
    b = _dot(x, sc_ref[...]).astype(BF16)
    o_ref[...] = (_dot(cl_ref[...], a) - _dot(sl_ref[...], b)).astype(BF16)


def _fourier(z, *, batch, seq, row_block_offset, col_block_offset, fg):
    cl, sl = _dft_tables(seq)
    cc, sc = _dft_tables(fg)
    tables = [jnp.asarray(t, BF16) for t in (cc, sc, cl, sl)]
    const = lambda n: pl.BlockSpec((n, n), lambda b, g: (0, 0))
    return pl.pallas_call(
        _fourier_kernel,
        grid=(batch, N_FGROUPS),
        in_specs=[pl.BlockSpec((seq, fg), lambda b, g: (row_block_offset + b, col_block_offset + g)),
                  const(fg), const(fg), const(seq), const(seq)],
        out_specs=pl.BlockSpec((seq, fg), lambda b, g: (b, g)),
        out_shape=jax.ShapeDtypeStruct((batch * seq, N_FGROUPS * fg), BF16),
        compiler_params=_params("arbitrary", "arbitrary"),
        name=f"fourier_{seq}",
    )(z, *tables)


def _outproj_kernel(retp_ref, rets_ref, fourp_ref, fours_ref, w_ref, xp_ref, xs_ref, mod_ref,
                    lnw_ref, lnb_ref, wr_ref, x1_ref, hu_ref, lg_ref, *, d, d_ret, alpha, rows):
    i = pl.program_id(0)

    def body(ret_ref, four_ref, x_ref):
        gate1 = mod_ref[0, :, 2 * d:3 * d]
        shift2 = mod_ref[0, :, 3 * d:4 * d]
        scale2 = mod_ref[0, :, 4 * d:5 * d]
        mix = _dot(ret_ref[...], w_ref[0:d_ret, :]) + _dot(four_ref[...], w_ref[d_ret:, :])
        x1 = _ln(alpha * x_ref[...] + gate1 * mix) * lnw_ref[...] + lnb_ref[...]
        x1_ref[...] = x1
        h2 = _ln(x1) * (1.0 + scale2) + shift2
        hu_ref[...] = _pack_halves(h2)
        wr = wr_ref[...]
        w_hi = wr.astype(BF16)
        w_lo = (wr - w_hi.astype(F32)).astype(BF16)
        h_hi = h2.astype(BF16)
        h_lo = (h2 - h_hi.astype(F32)).astype(BF16)
        lg_ref[...] = _dot_nt(w_hi, h_hi) + (_dot_nt(w_hi, h_lo) + _dot_nt(w_lo, h_hi))

    rows.when_prompt(i, lambda: body(retp_ref, fourp_ref, xp_ref))
    rows.when_latent(i, lambda: body(rets_ref, fours_ref, xs_ref))


def _outproj(retp, rets, fourp, fours, w_out_bf16, xp, xs, mod3, ln_w, ln_b, w_router_t, rows, alpha):
    d = xp.shape[1]
    d_ret = retp.shape[1]
    d_four = fourp.shape[1]
    tm = rows.tile
    t = rows.n_tiles * tm
    n_e = w_router_t.shape[0]
    prompt = lambda w: pl.BlockSpec((tm, w), lambda i: (rows.prompt_tile(i), 0))
    latent = lambda w: pl.BlockSpec((tm, w), lambda i: (rows.latent_tile(i), 0))
    const = lambda a: pl.BlockSpec(a.shape, lambda i: (0, 0))
    return pl.pallas_call(
        functools.partial(_outproj_kernel, d=d, d_ret=d_ret, alpha=alpha, rows=rows),
        grid=(rows.n_tiles,),
        in_specs=[
            prompt(d_ret), latent(d_ret), prompt(d_four), latent(d_four),
            const(w_out_bf16),
            prompt(d), latent(d),
            pl.BlockSpec((1, 1, mod3.shape[2]), lambda i: (rows.mod_row(i), 0, 0)),
            const(ln_w), const(ln_b), const(w_router_t),
        ],
        out_specs=[
            pl.BlockSpec((tm, d), lambda i: (i, 0)),
            pl.BlockSpec((tm, d // 2), lambda i: (i, 0)),
            pl.BlockSpec((n_e, tm), lambda i: (0, i)),
        ],
        out_shape=[
            jax.ShapeDtypeStruct((t, d), F32),
            jax.ShapeDtypeStruct((t, d // 2), U32),
            jax.ShapeDtypeStruct((n_e, t), F32),
        ],
        compiler_params=_params("arbitrary"),
        name="outproj",
    )(retp, rets, fourp, fours, w_out_bf16, xp, xs, mod3, ln_w, ln_b, w_router_t)


def _first_index_of_max(vals, index, sentinel, axis):
    m = jnp.max(vals, axis=axis, keepdims=True)
    return jnp.min(jnp.where(vals == m, index, sentinel), axis=axis, keepdims=True), m


def _router_kernel(lg_ref, bias_ref, eidx_ref, w_ref, rank_ref, cnt_ref, carry):
    i = pl.program_id(0)
    tr = lg_ref.shape[1]
    gsz = N_EXPERTS // N_GROUPS
    neg_inf = jnp.float32(-jnp.inf)

    @pl.when(i == 0)
    def _():
        carry[...] = jnp.zeros_like(carry)

    scores = jax.nn.sigmoid(lg_ref[...])
    biased = scores + bias_ref[...]
    b3 = biased.reshape(N_GROUPS, gsz, tr)
    sub = lax.broadcasted_iota(I32, (N_GROUPS, gsz, tr), 1)
    first, m1 = _first_index_of_max(b3, sub, gsz, 1)
    m2 = jnp.max(jnp.where(sub == first, neg_inf, b3), axis=1, keepdims=True)
    gscore = (m1 + m2).reshape(N_GROUPS, tr)

    gi = lax.broadcasted_iota(I32, (N_GROUPS, tr), 0)
    gsel = jnp.zeros((N_GROUPS, tr), jnp.bool_)
    cur = gscore
    for _ in range(TOPK_GROUP):
        first, _m = _first_index_of_max(cur, gi, N_GROUPS, 0)
        pick = gi == first
        gsel = jnp.logical_or(gsel, pick)
        cur = jnp.where(pick, neg_inf, cur)

    gsel3 = jnp.broadcast_to(gsel.reshape(N_GROUPS, 1, tr), (N_GROUPS, gsz, tr))
    cur = jnp.where(gsel3, b3, neg_inf).reshape(N_EXPERTS, tr)
    ei = lax.broadcasted_iota(I32, (N_EXPERTS, tr), 0)
    picks = []
    sel_scores = []
    for _ in range(TOP_K):
        first, _m = _first_index_of_max(cur, ei, N_EXPERTS, 0)
        pick = ei == first
        picks.append(first)
        sel_scores.append(jnp.sum(jnp.where(pick, scores, 0.0), axis=0, keepdims=True))
        cur = jnp.where(pick, neg_inf, cur)
    total = sel_scores[0]
    for sc in sel_scores[1:]:
        total = total + sc

    member = jnp.zeros((N_EXPERTS, tr), F32)
    for first in picks:
        member = member + jnp.where(ei == first, 1.0, 0.0)
    src = lax.broadcasted_iota(I32, (tr, tr), 0)
    dst = lax.broadcasted_iota(I32, (tr, tr), 1)
    upper = jnp.where(src < dst, 1.0, 0.0).astype(BF16)
    rank_all = _dot(member.astype(BF16), upper) + carry[...]
    for k in range(TOP_K):
        pick = ei == picks[k]
        eidx_ref[k:k + 1, :] = picks[k]
        w_ref[k:k + 1, :] = sel_scores[k] / total * ROUTED_SCALE
        rank_ref[k:k + 1, :] = jnp.sum(jnp.where(pick, rank_all, 0.0), axis=0, keepdims=True).astype(I32)
    carry[...] = carry[...] + jnp.sum(member, axis=1, keepdims=True)
    cnt_ref[...] = carry[...]


def _router(logits_t, bias):
    n_e, t = logits_t.shape
    tr = ROUTER_TILE
    tok = pl.BlockSpec((TOP_K, tr), lambda i: (0, i))
    return pl.pallas_call(
        _router_kernel,
        grid=(t // tr,),
        in_specs=[pl.BlockSpec((n_e, tr), lambda i: (0, i)),
                  pl.BlockSpec((n_e, 1), lambda i: (0, 0))],
        out_specs=[tok, tok, tok, pl.BlockSpec((n_e, 1), lambda i: (0, 0))],
        out_shape=[jax.ShapeDtypeStruct((TOP_K, t), I32),
                   jax.ShapeDtypeStruct((TOP_K, t), F32),
                   jax.ShapeDtypeStruct((TOP_K, t), I32),
                   jax.ShapeDtypeStruct((n_e, 1), F32)],
        scratch_shapes=[pltpu.VMEM((n_e, 1), F32)],
        compiler_params=_params("arbitrary"),
        name="router",
    )(logits_t, bias)


def _positions_kernel(start_ref, eidx_ref, rank_ref, pos_ref):
    e = eidx_ref[...]
    pos = rank_ref[...]
    for x in range(N_EXPERTS):
        pos = pos + jnp.where(e == x, start_ref[x], 0)
    pos_ref[...] = pos


def _positions(start, eidx, rank):
    full = pl.BlockSpec(eidx.shape, lambda i: (0, 0))
    return pl.pallas_call(
        _positions_kernel,
        grid=(1,),
        in_specs=[pl.BlockSpec(memory_space=pltpu.SMEM), full, full],
        out_specs=full,
        out_shape=jax.ShapeDtypeStruct(eidx.shape, I32),
        compiler_params=_params("arbitrary"),
        name="positions",
    )(start, eidx, rank)


def _scatter_row(src_ref, src_row, dst_ref, dst_row, sem):
    return pltpu.make_async_copy(src_ref.at[pl.ds(src_row, 1)], dst_ref.at[dst_row], sem)


def _gather_row(src_ref, src_row, dst_ref, dst_row, sem):
    return pltpu.make_async_copy(src_ref.at[src_row], dst_ref.at[dst_row], sem)


def _dispatch_kernel(pos_ref, h_ref, xs_ref, sem):
    td = h_ref.shape[0]

    def issue(t, carry):
        for k in range(TOP_K):
            _scatter_row(h_ref, t, xs_ref, pos_ref[k, t], sem).start(priority=k % N_DMA_PRIORITIES)
        return carry

    lax.fori_loop(0, td, issue, 0)

    def drain(t, carry):
        for k in range(TOP_K):
            _scatter_row(h_ref, 0, xs_ref, 0, sem).wait()
        return carry

    lax.fori_loop(0, td, drain, 0)


def _dispatch(pos, hu):
    t, w = hu.shape
    td = DISPATCH_TILE
    return pl.pallas_call(
        _dispatch_kernel,
        grid=(t // td,),
        in_specs=[pl.BlockSpec((TOP_K, td), lambda i: (0, i), memory_space=pltpu.SMEM),
                  pl.BlockSpec((td, w), lambda i: (i, 0))],
        out_specs=pl.BlockSpec(memory_space=pl.ANY),
        out_shape=jax.ShapeDtypeStruct((t * TOP_K, 1, w), hu.dtype),
        scratch_shapes=[pltpu.SemaphoreType.DMA],
        compiler_params=_params("arbitrary"),
        name="dispatch",
    )(pos, hu)


def _experts_kernel(blk_ref, exp_ref, lo_ref, hi_ref, np_ref, slot_ref, nxt_ref, x_ref,
                    wg_hbm, wu_hbm, wd_hbm, y_ref, wg_f, wu_f, wd_f, wg_s, wu_s, wd_s, x2_s, y2_s, sems):
    s = pl.program_id(0)
    prev = jnp.maximum(s - 1, 0)
    valid = s < np_ref[0]
    new_expert = jnp.logical_or(s == 0, exp_ref[s] != exp_ref[prev])
    first_visit = jnp.logical_or(s == 0, blk_ref[s] != blk_ref[prev])
    tm = x_ref.shape[0]

    def weight_copies(e, slot):
        return [pltpu.make_async_copy(hbm.at[e], buf.at[slot], sems.at[slot, m])
                for m, (hbm, buf) in enumerate(((wg_hbm, wg_f), (wu_hbm, wu_f), (wd_hbm, wd_f)))]

    @pl.when(s == 0)
    def _():
        for cp in weight_copies(exp_ref[0], slot_ref[0]):
            cp.start()

    @pl.when(jnp.logical_and(valid, new_expert))
    def _():
        slot = slot_ref[s]
        for cp in weight_copies(exp_ref[s], slot):
            cp.wait()

        @pl.when(nxt_ref[s] >= 0)
        def _():
            for cp in weight_copies(nxt_ref[s], 1 - slot):
                cp.start()

        wg_s[...] = wg_f[slot].astype(BF16)
        wu_s[...] = wu_f[slot].astype(BF16)
        wd_s[...] = wd_f[slot].astype(BF16)

    def compute():
        x2_s[...] = x_ref[...].reshape(x2_s.shape)
        lo, hi = _unpack_halves(x2_s[...])
        lo = lo.astype(BF16)
        hi = hi.astype(BF16)
        a = (_silu(_dot_halves(lo, hi, wg_s)) * _dot_halves(lo, hi, wu_s)).astype(BF16)
        y = _pack_halves(_dot(a, wd_s[...]))
        rows = blk_ref[s] * tm + lax.broadcasted_iota(I32, (tm, 1), 0)
        mine = jnp.logical_and(rows >= lo_ref[s], rows < hi_ref[s])
        return y, mine

    @pl.when(jnp.logical_and(valid, first_visit))
    def _():
        y, mine = compute()
        y2_s[...] = jnp.where(mine, y, jnp.uint32(0))

    @pl.when(jnp.logical_and(valid, jnp.logical_not(first_visit)))
    def _():
        y, mine = compute()
        y2_s[...] = jnp.where(mine, y, y2_s[...])

    @pl.when(valid)
    def _():
        y_ref[...] = y2_s[...].reshape(y_ref.shape)


def _experts(meta, xs, w_gate_e, w_up_e, w_down_e):
    n_rows, _, w = xs.shape
    d = w_gate_e.shape[1]
    d_e = w_gate_e.shape[2]
    tm = EXPERT_ROWS
    n_steps = meta[0].shape[0]
    rows_of_step = lambda s, blk, *_: (blk[s], 0, 0)
    hbm = pl.BlockSpec(memory_space=pl.ANY)
    grid_spec = pltpu.PrefetchScalarGridSpec(
        num_scalar_prefetch=len(meta),
        grid=(n_steps,),
        in_specs=[pl.BlockSpec((tm, 1, w), rows_of_step), hbm, hbm, hbm],
        out_specs=pl.BlockSpec((tm, 1, w), rows_of_step),
        scratch_shapes=[pltpu.VMEM((2, d, d_e), F32), pltpu.VMEM((2, d, d_e), F32), pltpu.VMEM((2, d_e, d), F32),
                        pltpu.VMEM((d, d_e), BF16), pltpu.VMEM((d, d_e), BF16), pltpu.VMEM((d_e, d), BF16),
                        pltpu.VMEM((tm, w), U32), pltpu.VMEM((tm, w), U32),
                        pltpu.SemaphoreType.DMA((2, 3))],
    )
    return pl.pallas_call(
        _experts_kernel,
        grid_spec=grid_spec,
        out_shape=jax.ShapeDtypeStruct((n_rows, 1, w), xs.dtype),
        compiler_params=_params("arbitrary"),
        name="experts",
    )(*meta, xs, w_gate_e, w_up_e, w_down_e)


def _lookup(table, idx):
    hit = idx[:, None] == jnp.arange(table.shape[0], dtype=I32)[None, :]
    return jnp.sum(jnp.where(hit, table[None, :], 0), axis=1)


def _expert_schedule(counts, n_rows):
    tm = EXPERT_ROWS
    n_blocks = n_rows // tm
    n_steps = n_blocks + N_EXPERTS - 1
    end = jnp.cumsum(counts)
    start = end - counts
    first_blk = start // tm
    last_blk = (end - 1) // tm
    n_blk = jnp.where(counts > 0, last_blk - first_blk + 1, 0)
    pair_end = jnp.cumsum(n_blk)
    pair_start = pair_end - n_blk
    n_pairs = pair_end[-1]
    step = jnp.minimum(jnp.arange(n_steps, dtype=I32), n_pairs - 1)
    exp = jnp.sum((pair_end[None, :] <= step[:, None]).astype(I32), axis=1)
    blk = _lookup(first_blk - pair_start, exp) + step
    active = n_blk > 0
    order = jnp.cumsum(active.astype(I32)) - 1
    ids = jnp.arange(N_EXPERTS, dtype=I32)
    later = jnp.where(jnp.logical_and(active[None, :], ids[None, :] > ids[:, None]), ids[None, :], N_EXPERTS)
    following = jnp.min(later, axis=1)
    following = jnp.where(following == N_EXPERTS, -1, following)
    meta = (blk, exp, _lookup(start, exp), _lookup(end, exp), n_pairs.reshape(1),
            _lookup(order % 2, exp), _lookup(following, exp))
    return meta, start


def _cast_kernel(x_ref, o_ref):
    o_ref[...] = x_ref[...].astype(o_ref.dtype)


def _cast_bf16(w):
    r, c = w.shape
    tile = min(r, 512)
    return pl.pallas_call(
        _cast_kernel,
        grid=(r // tile,),
        in_specs=[pl.BlockSpec((tile, c), lambda i: (i, 0))],
        out_specs=pl.BlockSpec((tile, c), lambda i: (i, 0)),
        out_shape=jax.ShapeDtypeStruct((r, c), BF16),
        compiler_params=_params("arbitrary"),
        name="cast_bf16",
    )(w)


def _combine_kernel(pos_ref, w_ref, hu_ref, x1_ref, mod_ref, lnw_ref, lnb_ref, wg_ref, wu_ref, wd_ref,
                    y_ref, op_ref, os_ref, gbuf, g2_s, sem, *, d, alpha, rows):
    i = pl.program_id(0)
    tc = hu_ref.shape[0]

    def issue(t, carry):
        for k in range(TOP_K):
            _gather_row(y_ref, pos_ref[k, t], gbuf, k * tc + t, sem).start(priority=k % N_DMA_PRIORITIES)
        return carry

    lax.fori_loop(0, tc, issue, 0)

    lo, hi = _unpack_halves(hu_ref[...])
    lo = lo.astype(BF16)
    hi = hi.astype(BF16)
    a = (_silu(_dot_halves(lo, hi, wg_ref)) * _dot_halves(lo, hi, wu_ref)).astype(BF16)
    shared = _dot(a, wd_ref[...])

    def drain(t, carry):
        for k in range(TOP_K):
            _gather_row(y_ref, 0, gbuf, 0, sem).wait()
        return carry

    lax.fori_loop(0, tc, drain, 0)

    g2_s[...] = gbuf[...].reshape(g2_s.shape)
    w = w_ref[...]
    r_lo = jnp.zeros((tc, d // 2), F32)
    r_hi = jnp.zeros((tc, d // 2), F32)
    for k in range(TOP_K):
        lo, hi = _unpack_halves(g2_s[k * tc:(k + 1) * tc, :])
        r_lo = r_lo + lo * w[:, k:k + 1]
        r_hi = r_hi + hi * w[:, k:k + 1]
    m = shared + jnp.concatenate([r_lo, r_hi], axis=1)
    gate2 = mod_ref[0, :, 5 * d:6 * d]
    out = _ln(alpha * x1_ref[...] + gate2 * m) * lnw_ref[...] + lnb_ref[...]

    def store(o_ref):
        o_ref[...] = out

    rows.when_prompt(i, lambda: store(op_ref))
    rows.when_latent(i, lambda: store(os_ref))


def _combine(pos, wts, hu, x1, mod3, ln_w, ln_b, wg, wu, wd, ybuf, rows, n_prompt_rows, alpha):
    t, d = x1.shape
    tc = rows.tile
    const = lambda a: pl.BlockSpec(a.shape, lambda i: (0, 0))
    return pl.pallas_call(
        functools.partial(_combine_kernel, d=d, alpha=alpha, rows=rows),
        grid=(rows.n_tiles,),
        in_specs=[
            pl.BlockSpec((TOP_K, tc), lambda i: (0, i), memory_space=pltpu.SMEM),
            pl.BlockSpec((tc, TOP_K), lambda i: (i, 0)),
            pl.BlockSpec((tc, d // 2), lambda i: (i, 0)),
            pl.BlockSpec((tc, d), lambda i: (i, 0)),
            pl.BlockSpec((1, 1, mod3.shape[2]), lambda i: (rows.mod_row(i), 0, 0)),
            const(ln_w), const(ln_b), const(wg), const(wu), const(wd),
            pl.BlockSpec(memory_space=pl.ANY),
        ],
        out_specs=[pl.BlockSpec((tc, d), lambda i: (rows.prompt_tile(i), 0)),
                   pl.BlockSpec((tc, d), lambda i: (rows.latent_tile(i), 0))],
        out_shape=[jax.ShapeDtypeStruct((n_prompt_rows, d), F32),
                   jax.ShapeDtypeStruct((t - n_prompt_rows, d), F32)],
        scratch_shapes=[pltpu.VMEM((TOP_K * tc, 1, d // 2), U32), pltpu.VMEM((TOP_K * tc, d // 2), U32),
                        pltpu.SemaphoreType.DMA],
        compiler_params=_params("arbitrary"),
        name="combine",
    )(pos, wts, hu, x1, mod3, ln_w, ln_b, wg, wu, wd, ybuf)


def kernel(x_prompt, x_sample, c, state_ret, c_ctx, w_ada, b_ada, w_in, ret_decay, ret_gn_w, w_out,
           ln1_w, ln1_b, w_router, router_bias, w_gate_e, w_up_e, w_down_e, w_gate_s, w_up_s, w_down_s,
           ln2_w, ln2_b):
    batch, seq, d = x_prompt.shape
    dec_batch, dec_seq, _ = x_sample.shape
    depth = w_in.shape[0]
    assert depth == 1
    n_prompt_rows = batch * seq
    n_latent_rows = dec_batch * dec_seq
    t = n_prompt_rows + n_latent_rows
    alpha = (2.0 * depth) ** 0.25
    d_ret = H_RET * CHUNK
    fg = (w_in.shape[2] - 4 * d_ret) // N_FGROUPS
    l = 0
    tiling = lambda tile: _Rows(tile, n_prompt_rows, n_latent_rows, dec_seq)

    xp = x_prompt.reshape(n_prompt_rows, d)
    xs = x_sample.reshape(n_latent_rows, d)

    n_mod = 1 + dec_batch
    cc = jnp.concatenate([c_ctx[None, :], c, jnp.zeros((-n_mod % 8, d), F32)], axis=0)
    mod = _ada(cc, w_ada[l], b_ada[l][None, :])
    mod3 = mod.reshape(mod.shape[0], 1, mod.shape[1])

    z = _inproj(xp, xs, mod3, w_in[l], tiling(ROW_TILE))

    log_g = jax.nn.log_sigmoid(ret_decay[l].astype(F32))
    gn_w = ret_gn_w[l][None, :]
    latent_row_block = n_prompt_rows // dec_seq
    ret_p, ctx_state = _retention(z, log_g, gn_w, batch=batch, seq=seq, row_block_offset=0, latent=False)
    ret_s = _retention(z, log_g, gn_w, batch=dec_batch, seq=dec_seq, row_block_offset=latent_row_block,
                       latent=True, state=state_ret[:, l])[0]
    fcol = 4 * d_ret // fg
    four_p = _fourier(z, batch=batch, seq=seq, row_block_offset=0, col_block_offset=fcol, fg=fg)
    four_s = _fourier(z, batch=dec_batch, seq=dec_seq, row_block_offset=latent_row_block,
                      col_block_offset=fcol, fg=fg)

    x1, hu, logits_t = _outproj(ret_p, ret_s, four_p, four_s, _cast_bf16(w_out[l]), xp, xs, mod3,
                                ln1_w[l][None, :], ln1_b[l][None, :], w_router[l].T,
                                tiling(OUT_ROW_TILE), alpha)

    eidx, wts, rank, counts = _router(logits_t, router_bias[l][:, None])
    meta, start = _expert_schedule(counts[:, 0].astype(I32), t * TOP_K)
    pos = _positions(start, eidx, rank)

    xsorted = _dispatch(pos, hu)
    ybuf = _experts(meta, xsorted, w_gate_e[l], w_up_e[l], w_down_e[l])
    y_prompt, y_sample = _combine(pos, wts.T, hu, x1, mod3, ln2_w[l][None, :], ln2_b[l][None, :],
                                  _cast_bf16(w_gate_s[l]), _cast_bf16(w_up_s[l]), _cast_bf16(w_down_s[l]),
                                  ybuf, tiling(COMBINE_TILE), n_prompt_rows, alpha)

    new_state = ctx_state[:, None].astype(x_prompt.dtype)
    return (y_prompt.reshape(batch, seq, d), y_sample.reshape(dec_batch, dec_seq, d), new_state)
```

```python
import functools

import numpy as np
import jax
import jax.numpy as jnp
from jax import lax
from jax.experimental import pallas as pl
from jax.experimental.pallas import tpu as pltpu

F32 = jnp.float32
BF16 = jnp.bfloat16
I32 = jnp.int32
U32 = jnp.uint32

GRID_W = 64
H_RET = 8
N_FGROUPS = 4
CHUNK = 128
ROPE_THETA = 10000.0
N_EXPERTS = 64
TOP_K = 8
N_GROUPS = 8
TOPK_GROUP = 4
ROUTED_SCALE = 2.5
LN_EPS = 1e-6
GN_EPS = 1e-5
N_DMA_PRIORITIES = 2

VMEM_LIMIT_BYTES = 56 * 1024 * 1024

ROW_TILE = 1024
IN_COL_TILE = 512
OUT_ROW_TILE = 256
ADA_COL_TILE = 1024
ROUTER_TILE = 512
DISPATCH_TILE = 256
EXPERT_ROWS = 256
COMBINE_TILE = 256
LN_ROWS = 256
RET_HEADS_PER_STEP = 4
TOKENS_PER_ISSUE_ITER = 4


def _params(*sem):
    return pltpu.CompilerParams(dimension_semantics=sem, vmem_limit_bytes=VMEM_LIMIT_BYTES)


def _ln(x):
    mu = jnp.mean(x, axis=-1, keepdims=True)
    xc = x - mu
    var = jnp.mean(xc * xc, axis=-1, keepdims=True)
    return xc * lax.rsqrt(var + LN_EPS)


def _silu(x):
    return x * jax.nn.sigmoid(x)


def _dot(a, b):
    return jnp.dot(a, b, preferred_element_type=F32)


def _dot_nt(a, b):
    return lax.dot_general(a, b, (((1,), (1,)), ((), ())), preferred_element_type=F32)


def _dot_tn(a, b):
    return lax.dot_general(a, b, (((0,), (0,)), ((), ())), preferred_element_type=F32)


def _pack_halves(x):
    n = x.shape[1] // 2
    lo = lax.bitcast_convert_type(x[:, :n].astype(BF16).astype(F32), U32)
    hi = lax.bitcast_convert_type(x[:, n:].astype(BF16).astype(F32), U32)
    return (lo >> 16) | hi


def _unpack_halves(u):
    lo = lax.bitcast_convert_type(u << 16, F32)
    hi = lax.bitcast_convert_type(u & jnp.uint32(0xFFFF0000), F32)
    return lo, hi


def _dot_halves(lo, hi, w_ref):
    n = lo.shape[1]
    return _dot(lo, w_ref[0:n, :]) + _dot(hi, w_ref[n:2 * n, :])


def _ada_kernel(c_ref, w_ref, b_ref, o_ref):
    s = _silu(c_ref[...]).astype(BF16)
    o_ref[...] = _dot(s, w_ref[...].astype(BF16)) + b_ref[...]


def _ada(cc, w_ada, b_ada):
    rows, d = cc.shape
    n = w_ada.shape[1]
    return pl.pallas_call(
        _ada_kernel,
        grid=(n // ADA_COL_TILE,),
        in_specs=[
            pl.BlockSpec((rows, d), lambda j: (0, 0)),
            pl.BlockSpec((d, ADA_COL_TILE), lambda j: (0, j)),
            pl.BlockSpec((1, ADA_COL_TILE), lambda j: (0, j)),
        ],
        out_specs=pl.BlockSpec((rows, ADA_COL_TILE), lambda j: (0, j)),
        out_shape=jax.ShapeDtypeStruct((rows, n), F32),
        compiler_params=_params("arbitrary"),
        name="ada",
    )(cc, w_ada, b_ada)


class _Rows:
    def __init__(self, tile, n_prompt_rows, n_latent_rows, dec_seq):
        assert n_prompt_rows % tile == 0 and n_latent_rows % tile == 0 and dec_seq % tile == 0
        self.tile = tile
        self.n_prompt_tiles = n_prompt_rows // tile
        self.n_tiles = (n_prompt_rows + n_latent_rows) // tile
        self.tiles_per_latent_batch = dec_seq // tile

    def prompt_tile(self, i):
        return jnp.minimum(i, self.n_prompt_tiles - 1)

    def latent_tile(self, i):
        return jnp.maximum(i - self.n_prompt_tiles, 0)

    def mod_row(self, i):
        return jnp.where(i < self.n_prompt_tiles, 0,
                         1 + (i - self.n_prompt_tiles) // self.tiles_per_latent_batch)

    def when_prompt(self, i, fn, extra=True):
        pl.when(jnp.logical_and(extra, i < self.n_prompt_tiles))(fn)

    def when_latent(self, i, fn, extra=True):
        pl.when(jnp.logical_and(extra, i >= self.n_prompt_tiles))(fn)


def _inproj_kernel(xp_ref, xs_ref, mod_ref, w_ref, z_ref, h_scr, *, d, rows):
    i = pl.program_id(0)
    first = pl.program_id(1) == 0

    def fill(x_ref):
        shift = mod_ref[0, :, 0:d]
        scale = mod_ref[0, :, d:2 * d]
        for r in range(0, x_ref.shape[0], LN_ROWS):
            rs = slice(r, r + LN_ROWS)
            h_scr[rs, :] = (_ln(x_ref[rs, :]) * (1.0 + scale) + shift).astype(BF16)

    rows.when_prompt(i, lambda: fill(xp_ref), first)
    rows.when_latent(i, lambda: fill(xs_ref), first)
    z_ref[...] = _dot(h_scr[...], w_ref[...].astype(BF16)).astype(BF16)


def _inproj(xp, xs, mod3, w_in, rows):
    d = xp.shape[1]
    n = w_in.shape[1]
    tm = rows.tile
    return pl.pallas_call(
        functools.partial(_inproj_kernel, d=d, rows=rows),
        grid=(rows.n_tiles, n // IN_COL_TILE),
        in_specs=[
            pl.BlockSpec((tm, d), lambda i, j: (rows.prompt_tile(i), 0)),
            pl.BlockSpec((tm, d), lambda i, j: (rows.latent_tile(i), 0)),
            pl.BlockSpec((1, 1, mod3.shape[2]), lambda i, j: (rows.mod_row(i), 0, 0)),
            pl.BlockSpec((d, IN_COL_TILE), lambda i, j: (0, j)),
        ],
        out_specs=pl.BlockSpec((tm, IN_COL_TILE), lambda i, j: (i, j)),
        out_shape=jax.ShapeDtypeStruct((rows.n_tiles * tm, n), BF16),
        scratch_shapes=[pltpu.VMEM((tm, d), BF16)],
        compiler_params=_params("arbitrary", "arbitrary"),
        name="inproj",
    )(xp, xs, mod3, w_in)


def _rope_tables(seq):
    dh = CHUNK
    nf = dh // 4
    pos = np.arange(seq)
    row = (pos // GRID_W).astype(np.float64)
    col = (pos % GRID_W).astype(np.float64)
    freqs = ROPE_THETA ** (-np.arange(nf, dtype=np.float64) / nf)
    ar = row[:, None] * freqs[None, :]
    ac = col[:, None] * freqs[None, :]
    cos = np.concatenate([np.cos(ar), np.cos(ar), np.cos(ac), np.cos(ac)], axis=1)
    sin = np.concatenate([-np.sin(ar), np.sin(ar), -np.sin(ac), np.sin(ac)], axis=1)
    return jnp.asarray(cos, F32), jnp.asarray(sin, F32)


def _retention_kernel(*refs, seq, latent, k_scale):
    if latent:
        (lg_ref, q_ref, k_ref, v_ref, g_ref, gnw_ref, cos_ref, sin_ref, s0_ref,
         o_ref, q_scr, k_scr, o_scr) = refs
        st_ref = None
    else:
        (lg_ref, q_ref, k_ref, v_ref, g_ref, gnw_ref,
         o_ref, st_ref, q_scr, k_scr, o_scr) = refs
        s0_ref = cos_ref = sin_ref = None
    c = CHUNK
    n = seq // c
    heads = range(RET_HEADS_PER_STEP)
    h0 = pl.program_id(1) * RET_HEADS_PER_STEP
    cols = lambda hh: slice(hh * c, (hh + 1) * c)

    ii = lax.broadcasted_iota(I32, (c, c), 0).astype(F32)
    jj = lax.broadcasted_iota(I32, (c, c), 1).astype(F32)
    diff = ii - jj
    neg_inf = jnp.float32(-jnp.inf)
    idx = lax.broadcasted_iota(I32, (c, 1), 0).astype(F32)
    ones_row = jnp.ones((1, c), F32)

    def decays(hh):
        lgf = lg_ref[0, h0 + hh]
        lgb = lg_ref[1, h0 + hh]
        return dict(
            dmask=(jnp.exp(jnp.where(diff >= 0, diff * lgf, neg_inf))
                   + jnp.exp(jnp.where(diff <= 0, (-diff) * lgb, neg_inf))),
            qdec_f=jnp.exp((idx + 1.0) * lgf), kdec_f=jnp.exp((c - 1.0 - idx) * lgf),
            qdec_b=jnp.exp((c - idx) * lgb), kdec_b=jnp.exp(idx * lgb),
            cdec_f=jnp.exp(ones_row * (c * lgf)), cdec_b=jnp.exp(ones_row * (c * lgb)))

    dec = [decays(hh) for hh in heads]

    if latent:
        lane = lax.broadcasted_iota(I32, (c, c), 1)
        first_quarter = (lane % (c // 2)) < (c // 4)

        def rope(x, rows):
            partner = jnp.where(first_quarter, pltpu.roll(x, c - c // 4, 1), pltpu.roll(x, c // 4, 1))
            return x * cos_ref[rows, :] + partner * sin_ref[rows, :]

    for ci in range(n):
        rows = slice(ci * c, (ci + 1) * c)
        for hh in heads:
            q = q_ref[rows, cols(hh)].astype(F32)
            k = k_ref[rows, cols(hh)].astype(F32) * k_scale
            if latent:
                q = rope(q, rows)
                k = rope(k, rows)
            q_scr[rows, cols(hh)] = q
            k_scr[rows, cols(hh)] = k
            scores = _dot_nt(q.astype(BF16), k.astype(BF16)) * dec[hh]["dmask"]
            o_scr[rows, cols(hh)] = _dot(scores.astype(BF16), v_ref[rows, cols(hh)])

    s = [s0_ref[0, hh] if latent else jnp.zeros((c, c), F32) for hh in heads]
    for ci in range(n):
        rows = slice(ci * c, (ci + 1) * c)
        for hh in heads:
            q = q_scr[rows, cols(hh)]
            k = k_scr[rows, cols(hh)]
            o_scr[rows, cols(hh)] += _dot((q * dec[hh]["qdec_f"]).astype(BF16), s[hh].astype(BF16))
            s[hh] = s[hh] * dec[hh]["cdec_f"] + _dot_tn((k * dec[hh]["kdec_f"]).astype(BF16),
                                                          v_ref[rows, cols(hh)])
    if st_ref is not None:
        for hh in heads:
            st_ref[0, hh] = s[hh]

    s = [s0_ref[1, hh] if latent else jnp.zeros((c, c), F32) for hh in heads]
    for ci in reversed(range(n)):
        rows = slice(ci * c, (ci + 1) * c)
        for hh in heads:
            q = q_scr[rows, cols(hh)]
            k = k_scr[rows, cols(hh)]
            o = o_scr[rows, cols(hh)] + _dot((q * dec[hh]["qdec_b"]).astype(BF16), s[hh].astype(BF16))
            s[hh] = s[hh] * dec[hh]["cdec_b"] + _dot_tn((k * dec[hh]["kdec_b"]).astype(BF16),
                                                          v_ref[rows, cols(hh)])
            mu = jnp.mean(o, axis=-1, keepdims=True)
            oc = o - mu
            var = jnp.mean(oc * oc, axis=-1, keepdims=True)
            on = oc * lax.rsqrt(var + GN_EPS) * gnw_ref[:, cols(hh)]
            o_ref[rows, cols(hh)] = (_silu(g_ref[rows, cols(hh)].astype(F32)) * on).astype(BF16)
    if st_ref is not None:
        for hh in heads:
            st_ref[1, hh] = s[hh]


def _retention(z, log_g, gn_w, *, batch, seq, row_block_offset, latent, state=None):
    dh = CHUNK
    hp = RET_HEADS_PER_STEP
    d_ret = H_RET * dh
    zspec = lambda part: pl.BlockSpec((seq, hp * dh),
                                      lambda b, h: (row_block_offset + b, part * (H_RET // hp) + h))
    in_specs = [
        pl.BlockSpec(memory_space=pltpu.SMEM),
        zspec(0), zspec(1), zspec(2), zspec(3),
        pl.BlockSpec((1, hp * dh), lambda b, h: (0, h)),
    ]
    args = [log_g, z, z, z, z, gn_w]
    out_specs = [pl.BlockSpec((seq, hp * dh), lambda b, h: (b, h))]
    out_shape = [jax.ShapeDtypeStruct((batch * seq, d_ret), BF16)]
    state_spec = pl.BlockSpec((None, 2, hp, dh, dh), lambda b, h: (b, 0, h, 0, 0))
    if latent:
        cos, sin = _rope_tables(seq)
        in_specs += [pl.BlockSpec((seq, dh), lambda b, h: (0, 0)),
                     pl.BlockSpec((seq, dh), lambda b, h: (0, 0)),
                     state_spec]
        args += [cos, sin, state]
    else:
        out_specs.append(state_spec)
        out_shape.append(jax.ShapeDtypeStruct((batch, 2, H_RET, dh, dh), F32))
    return pl.pallas_call(
        functools.partial(_retention_kernel, seq=seq, latent=latent, k_scale=dh ** -0.5),
        grid=(batch, H_RET // hp),
        in_specs=in_specs,
        out_specs=out_specs,
        out_shape=out_shape,
        scratch_shapes=[pltpu.VMEM((seq, hp * dh), F32)] * 3,
        compiler_params=_params("arbitrary", "arbitrary"),
        name="retention_latent" if latent else "retention_context",
    )(*args)


def _dft_tables(n):
    jk = np.outer(np.arange(n), np.arange(n)) % n
    ang = 2.0 * np.pi * jk / n
    return np.cos(ang) / np.sqrt(n), np.sin(ang) / np.sqrt(n)


def _fourier_kernel(x_ref, chan_ref, posn_ref, o_ref, *, fg):
    for g in range(N_FGROUPS):
        cols = slice(g * fg, (g + 1) * fg)
        ab = _dot(x_ref[:, cols], chan_ref[...]).astype(BF16)
        stacked = jnp.concatenate([ab[:, :fg], ab[:, fg:]], axis=0)
        o_ref[:, cols] = _dot(posn_ref[...], stacked).astype(BF16)


def _fourier(z, *, batch, seq, row_block_offset, col_block_offset, fg):
    cl, sl = _dft_tables(seq)
    cc, sc = _dft_tables(fg)
    chan = jnp.asarray(np.concatenate([cc, sc], axis=1), BF16)
    posn = jnp.asarray(np.concatenate([cl, -sl], axis=1), BF16)
    width = N_FGROUPS * fg
    assert (col_block_offset * fg) % width == 0
    return pl.pallas_call(
        functools.partial(_fourier_kernel, fg=fg),
        grid=(batch,),
        in_specs=[pl.BlockSpec((seq, width), lambda b: (row_block_offset + b, col_block_offset * fg // width)),
                  pl.BlockSpec(chan.shape, lambda b: (0, 0)),
                  pl.BlockSpec(posn.shape, lambda b: (0, 0))],
        out_specs=pl.BlockSpec((seq, width), lambda b: (b, 0)),
        out_shape=jax.ShapeDtypeStruct((batch * seq, width), BF16),
        compiler_params=_params("arbitrary"),
        name=f"fourier_{seq}",
    )(z, chan, posn)


def _outproj_kernel(retp_ref, rets_ref, fourp_ref, fours_ref, w_ref, xp_ref, xs_ref, mod_ref,
                    lnw_ref, lnb_ref, wr_ref, x1_ref, hu_ref, lg_ref, *, d, d_ret, alpha, rows):
    i = pl.program_id(0)

    def body(ret_ref, four_ref, x_ref):
        gate1 = mod_ref[0, :, 2 * d:3 * d]
        shift2 = mod_ref[0, :, 3 * d:4 * d]
        scale2 = mod_ref[0, :, 4 * d:5 * d]
        mix = _dot(ret_ref[...], w_ref[0:d_ret, :]) + _dot(four_ref[...], w_ref[d_ret:, :])
        x1 = _ln(alpha * x_ref[...] + gate1 * mix) * lnw_ref[...] + lnb_ref[...]
        x1_ref[...] = x1
        h2 = _ln(x1) * (1.0 + scale2) + shift2
        hu_ref[...] = _pack_halves(h2)
        wr = wr_ref[...]
        w_hi = wr.astype(BF16)
        w_lo = (wr - w_hi.astype(F32)).astype(BF16)
        h_hi = h2.astype(BF16)
        h_lo = (h2 - h_hi.astype(F32)).astype(BF16)
        lg_ref[...] = _dot_nt(w_hi, h_hi) + (_dot_nt(w_hi, h_lo) + _dot_nt(w_lo, h_hi))

    rows.when_prompt(i, lambda: body(retp_ref, fourp_ref, xp_ref))
    rows.when_latent(i, lambda: body(rets_ref, fours_ref, xs_ref))


def _outproj(retp, rets, fourp, fours, w_out_bf16, xp, xs, mod3, ln_w, ln_b, w_router_t, rows, alpha):
    d = xp.shape[1]
    d_ret = retp.shape[1]
    d_four = fourp.shape[1]
    tm = rows.tile
    t = rows.n_tiles * tm
    n_e = w_router_t.shape[0]
    prompt = lambda w: pl.BlockSpec((tm, w), lambda i: (rows.prompt_tile(i), 0))
    latent = lambda w: pl.BlockSpec((tm, w), lambda i: (rows.latent_tile(i), 0))
    const = lambda a: pl.BlockSpec(a.shape, lambda i: (0, 0))
    return pl.pallas_call(
        functools.partial(_outproj_kernel, d=d, d_ret=d_ret, alpha=alpha, rows=rows),
        grid=(rows.n_tiles,),
        in_specs=[
            prompt(d_ret), latent(d_ret), prompt(d_four), latent(d_four),
            const(w_out_bf16),
            prompt(d), latent(d),
            pl.BlockSpec((1, 1, mod3.shape[2]), lambda i: (rows.mod_row(i), 0, 0)),
            const(ln_w), const(ln_b), const(w_router_t),
        ],
        out_specs=[
            pl.BlockSpec((tm, d), lambda i: (i, 0)),
            pl.BlockSpec((tm, d // 2), lambda i: (i, 0)),
            pl.BlockSpec((n_e, tm), lambda i: (0, i)),
        ],
        out_shape=[
            jax.ShapeDtypeStruct((t, d), F32),
            jax.ShapeDtypeStruct((t, d // 2), U32),
            jax.ShapeDtypeStruct((n_e, t), F32),
        ],
        compiler_params=_params("arbitrary"),
        name="outproj",
    )(retp, rets, fourp, fours, w_out_bf16, xp, xs, mod3, ln_w, ln_b, w_router_t)


def _first_index_of_max(vals, index, sentinel, axis):
    m = jnp.max(vals, axis=axis, keepdims=True)
    return jnp.min(jnp.where(vals == m, index, sentinel), axis=axis, keepdims=True), m


def _router_kernel(lg_ref, bias_ref, eidx_ref, w_ref, rank_ref, cnt_ref, carry):
    i = pl.program_id(0)
    tr = lg_ref.shape[1]
    gsz = N_EXPERTS // N_GROUPS
    neg_inf = jnp.float32(-jnp.inf)

    @pl.when(i == 0)
    def _():
        carry[...] = jnp.zeros_like(carry)

    scores = jax.nn.sigmoid(lg_ref[...])
    biased = scores + bias_ref[...]
    b3 = biased.reshape(N_GROUPS, gsz, tr)
    sub = lax.broadcasted_iota(I32, (N_GROUPS, gsz, tr), 1)
    first, m1 = _first_index_of_max(b3, sub, gsz, 1)
    m2 = jnp.max(jnp.where(sub == first, neg_inf, b3), axis=1, keepdims=True)
    gscore = (m1 + m2).reshape(N_GROUPS, tr)

    gi = lax.broadcasted_iota(I32, (N_GROUPS, tr), 0)
    gsel = jnp.zeros((N_GROUPS, tr), jnp.bool_)
    cur = gscore
    for _ in range(TOPK_GROUP):
        first, _m = _first_index_of_max(cur, gi, N_GROUPS, 0)
        pick = gi == first
        gsel = jnp.logical_or(gsel, pick)
        cur = jnp.where(pick, neg_inf, cur)

    gsel3 = jnp.broadcast_to(gsel.reshape(N_GROUPS, 1, tr), (N_GROUPS, gsz, tr))
    cur = jnp.where(gsel3, b3, neg_inf).reshape(N_EXPERTS, tr)
    ei = lax.broadcasted_iota(I32, (N_EXPERTS, tr), 0)
    picks = []
    sel_scores = []
    for _ in range(TOP_K):
        first, _m = _first_index_of_max(cur, ei, N_EXPERTS, 0)
        pick = ei == first
        picks.append(first)
        sel_scores.append(jnp.sum(jnp.where(pick, scores, 0.0), axis=0, keepdims=True))
        cur = jnp.where(pick, neg_inf, cur)
    total = sel_scores[0]
    for sc in sel_scores[1:]:
        total = total + sc

    member = jnp.zeros((N_EXPERTS, tr), F32)
    for first in picks:
        member = member + jnp.where(ei == first, 1.0, 0.0)
    src = lax.broadcasted_iota(I32, (tr, tr), 0)
    dst = lax.broadcasted_iota(I32, (tr, tr), 1)
    upper = jnp.where(src < dst, 1.0, 0.0).astype(BF16)
    rank_all = _dot(member.astype(BF16), upper) + carry[...]
    for k in range(TOP_K):
        pick = ei == picks[k]
        eidx_ref[k:k + 1, :] = picks[k]
        w_ref[k:k + 1, :] = sel_scores[k] / total * ROUTED_SCALE
        rank_ref[k:k + 1, :] = jnp.sum(jnp.where(pick, rank_all, 0.0), axis=0, keepdims=True).astype(I32)
    carry[...] = carry[...] + jnp.sum(member, axis=1, keepdims=True)
    cnt_ref[...] = carry[...]


def _router(logits_t, bias):
    n_e, t = logits_t.shape
    tr = ROUTER_TILE
    tok = pl.BlockSpec((TOP_K, tr), lambda i: (0, i))
    return pl.pallas_call(
        _router_kernel,
        grid=(t // tr,),
        in_specs=[pl.BlockSpec((n_e, tr), lambda i: (0, i)),
                  pl.BlockSpec((n_e, 1), lambda i: (0, 0))],
        out_specs=[tok, tok, tok, pl.BlockSpec((n_e, 1), lambda i: (0, 0))],
        out_shape=[jax.ShapeDtypeStruct((TOP_K, t), I32),
                   jax.ShapeDtypeStruct((TOP_K, t), F32),
                   jax.ShapeDtypeStruct((TOP_K, t), I32),
                   jax.ShapeDtypeStruct((n_e, 1), F32)],
        scratch_shapes=[pltpu.VMEM((n_e, 1), F32)],
        compiler_params=_params("arbitrary"),
        name="router",
    )(logits_t, bias)


def _positions_kernel(start_ref, eidx_ref, rank_ref, pos_ref):
    e = eidx_ref[...]
    pos = rank_ref[...]
    for x in range(N_EXPERTS):
        pos = pos + jnp.where(e == x, start_ref[x], 0)
    pos_ref[...] = pos


def _positions(start, eidx, rank):
    full = pl.BlockSpec(eidx.shape, lambda i: (0, 0))
    return pl.pallas_call(
        _positions_kernel,
        grid=(1,),
        in_specs=[pl.BlockSpec(memory_space=pltpu.SMEM), full, full],
        out_specs=full,
        out_shape=jax.ShapeDtypeStruct(eidx.shape, I32),
        compiler_params=_params("arbitrary"),
        name="positions",
    )(start, eidx, rank)


def _scatter_row(src_ref, src_row, dst_ref, dst_row, sem):
    return pltpu.make_async_copy(src_ref.at[pl.ds(src_row, 1)], dst_ref.at[dst_row], sem)


def _gather_row(src_ref, src_row, dst_ref, dst_row, sem):
    return pltpu.make_async_copy(src_ref.at[src_row], dst_ref.at[dst_row], sem)


def _dispatch_kernel(pos_ref, h_ref, xs_ref, sem):
    td = h_ref.shape[0]

    def issue(it, carry):
        for u in range(TOKENS_PER_ISSUE_ITER):
            t = it * TOKENS_PER_ISSUE_ITER + u
            for k in range(TOP_K):
                _scatter_row(h_ref, t, xs_ref, pos_ref[k, t], sem).start(priority=k % N_DMA_PRIORITIES)
        return carry

    lax.fori_loop(0, td // TOKENS_PER_ISSUE_ITER, issue, 0)

    def drain(t, carry):
        for k in range(TOP_K):
            _scatter_row(h_ref, 0, xs_ref, 0, sem).wait()
        return carry

    lax.fori_loop(0, td, drain, 0)


def _dispatch(pos, hu):
    t, w = hu.shape
    td = DISPATCH_TILE
    return pl.pallas_call(
        _dispatch_kernel,
        grid=(t // td,),
        in_specs=[pl.BlockSpec((TOP_K, td), lambda i: (0, i), memory_space=pltpu.SMEM),
                  pl.BlockSpec((td, w), lambda i: (i, 0))],
        out_specs=pl.BlockSpec(memory_space=pl.ANY),
        out_shape=jax.ShapeDtypeStruct((t * TOP_K, 1, w), hu.dtype),
        scratch_shapes=[pltpu.SemaphoreType.DMA],
        compiler_params=_params("arbitrary"),
        name="dispatch",
    )(pos, hu)


def _experts_kernel(blk_ref, exp_ref, lo_ref, hi_ref, np_ref, slot_ref, nxt_ref, x_ref,
                    wg_hbm, wu_hbm, wd_hbm, y_ref, wg_f, wu_f, wd_f, wg_s, wu_s, wd_s, x2_s, y2_s, sems):
    s = pl.program_id(0)
    prev = jnp.maximum(s - 1, 0)
    valid = s < np_ref[0]
    new_expert = jnp.logical_or(s == 0, exp_ref[s] != exp_ref[prev])
    first_visit = jnp.logical_or(s == 0, blk_ref[s] != blk_ref[prev])
    tm = x_ref.shape[0]

    def weight_copies(e, slot):
        return [pltpu.make_async_copy(hbm.at[e], buf.at[slot], sems.at[slot, m])
                for m, (hbm, buf) in enumerate(((wg_hbm, wg_f), (wu_hbm, wu_f), (wd_hbm, wd_f)))]

    @pl.when(s == 0)
    def _():
        for cp in weight_copies(exp_ref[0], slot_ref[0]):
            cp.start()

    @pl.when(jnp.logical_and(valid, new_expert))
    def _():
        slot = slot_ref[s]
        for cp in weight_copies(exp_ref[s], slot):
            cp.wait()

        @pl.when(nxt_ref[s] >= 0)
        def _():
            for cp in weight_copies(nxt_ref[s], 1 - slot):
                cp.start()

        wg_s[...] = wg_f[slot].astype(BF16)
        wu_s[...] = wu_f[slot].astype(BF16)
        wd_s[...] = wd_f[slot].astype(BF16)

    def compute():
        x2_s[...] = x_ref[...].reshape(x2_s.shape)
        lo, hi = _unpack_halves(x2_s[...])
        lo = lo.astype(BF16)
        hi = hi.astype(BF16)
        a = (_silu(_dot_halves(lo, hi, wg_s)) * _dot_halves(lo, hi, wu_s)).astype(BF16)
        y = _pack_halves(_dot(a, wd_s[...]))
        rows = blk_ref[s] * tm + lax.broadcasted_iota(I32, (tm, 1), 0)
        mine = jnp.logical_and(rows >= lo_ref[s], rows < hi_ref[s])
        return y, mine

    @pl.when(jnp.logical_and(valid, first_visit))
    def _():
        y, mine = compute()
        y2_s[...] = jnp.where(mine, y, jnp.uint32(0))

    @pl.when(jnp.logical_and(valid, jnp.logical_not(first_visit)))
    def _():
        y, mine = compute()
        y2_s[...] = jnp.where(mine, y, y2_s[...])

    @pl.when(valid)
    def _():
        y_ref[...] = y2_s[...].reshape(y_ref.shape)


def _experts(meta, xs, w_gate_e, w_up_e, w_down_e):
    n_rows, _, w = xs.shape
    d = w_gate_e.shape[1]
    d_e = w_gate_e.shape[2]
    tm = EXPERT_ROWS
    n_steps = meta[0].shape[0]
    rows_of_step = lambda s, blk, *_: (blk[s], 0, 0)
    hbm = pl.BlockSpec(memory_space=pl.ANY)
    grid_spec = pltpu.PrefetchScalarGridSpec(
        num_scalar_prefetch=len(meta),
        grid=(n_steps,),
        in_specs=[pl.BlockSpec((tm, 1, w), rows_of_step), hbm, hbm, hbm],
        out_specs=pl.BlockSpec((tm, 1, w), rows_of_step),
        scratch_shapes=[pltpu.VMEM((2, d, d_e), F32), pltpu.VMEM((2, d, d_e), F32), pltpu.VMEM((2, d_e, d), F32),
                        pltpu.VMEM((d, d_e), BF16), pltpu.VMEM((d, d_e), BF16), pltpu.VMEM((d_e, d), BF16),
                        pltpu.VMEM((tm, w), U32), pltpu.VMEM((tm, w), U32),
                        pltpu.SemaphoreType.DMA((2, 3))],
    )
    return pl.pallas_call(
        _experts_kernel,
        grid_spec=grid_spec,
        out_shape=jax.ShapeDtypeStruct((n_rows, 1, w), xs.dtype),
        compiler_params=_params("arbitrary"),
        name="experts",
    )(*meta, xs, w_gate_e, w_up_e, w_down_e)


def _lookup(table, idx):
    hit = idx[:, None] == jnp.arange(table.shape[0], dtype=I32)[None, :]
    return jnp.sum(jnp.where(hit, table[None, :], 0), axis=1)


def _expert_schedule(counts, n_rows):
    tm = EXPERT_ROWS
    n_blocks = n_rows // tm
    n_steps = n_blocks + N_EXPERTS - 1
    end = jnp.cumsum(counts)
    start = end - counts
    first_blk = start // tm
    last_blk = (end - 1) // tm
    n_blk = jnp.where(counts > 0, last_blk - first_blk + 1, 0)
    pair_end = jnp.cumsum(n_blk)
    pair_start = pair_end - n_blk
    n_pairs = pair_end[-1]
    step = jnp.minimum(jnp.arange(n_steps, dtype=I32), n_pairs - 1)
    exp = jnp.sum((pair_end[None, :] <= step[:, None]).astype(I32), axis=1)
    blk = _lookup(first_blk - pair_start, exp) + step
    active = n_blk > 0
    order = jnp.cumsum(active.astype(I32)) - 1
    ids = jnp.arange(N_EXPERTS, dtype=I32)
    later = jnp.where(jnp.logical_and(active[None, :], ids[None, :] > ids[:, None]), ids[None, :], N_EXPERTS)
    following = jnp.min(later, axis=1)
    following = jnp.where(following == N_EXPERTS, -1, following)
    meta = (blk, exp, _lookup(start, exp), _lookup(end, exp), n_pairs.reshape(1),
            _lookup(order % 2, exp), _lookup(following, exp))
    return meta, start


def _cast_kernel(x_ref, o_ref):
    o_ref[...] = x_ref[...].astype(o_ref.dtype)


def _cast_bf16(w):
    r, c = w.shape
    tile = min(r, 512)
    return pl.pallas_call(
        _cast_kernel,
        grid=(r // tile,),
        in_specs=[pl.BlockSpec((tile, c), lambda i: (i, 0))],
        out_specs=pl.BlockSpec((tile, c), lambda i: (i, 0)),
        out_shape=jax.ShapeDtypeStruct((r, c), BF16),
        compiler_params=_params("arbitrary"),
        name="cast_bf16",
    )(w)


def _combine_kernel(pos_ref, w_ref, hu_ref, x1_ref, mod_ref, lnw_ref, lnb_ref, wg_ref, wu_ref, wd_ref,
                    y_ref, op_ref, os_ref, gbuf, g2_s, sem, *, d, alpha, rows):
    i = pl.program_id(0)
    tc = hu_ref.shape[0]

    def issue(it, carry):
        for u in range(TOKENS_PER_ISSUE_ITER):
            t = it * TOKENS_PER_ISSUE_ITER + u
            for k in range(TOP_K):
                _gather_row(y_ref, pos_ref[k, t], gbuf, k * tc + t, sem).start(priority=k % N_DMA_PRIORITIES)
        return carry

    lax.fori_loop(0, tc // TOKENS_PER_ISSUE_ITER, issue, 0)

    lo, hi = _unpack_halves(hu_ref[...])
    lo = lo.astype(BF16)
    hi = hi.astype(BF16)
    a = (_silu(_dot_halves(lo, hi, wg_ref)) * _dot_halves(lo, hi, wu_ref)).astype(BF16)
    shared = _dot(a, wd_ref[...])

    def drain(t, carry):
        for k in range(TOP_K):
            _gather_row(y_ref, 0, gbuf, 0, sem).wait()
        return carry

    lax.fori_loop(0, tc, drain, 0)

    g2_s[...] = gbuf[...].reshape(g2_s.shape)
    w = w_ref[...]
    r_lo = jnp.zeros((tc, d // 2), F32)
    r_hi = jnp.zeros((tc, d // 2), F32)
    for k in range(TOP_K):
        lo, hi = _unpack_halves(g2_s[k * tc:(k + 1) * tc, :])
        r_lo = r_lo + lo * w[:, k:k + 1]
        r_hi = r_hi + hi * w[:, k:k + 1]
    m = shared + jnp.concatenate([r_lo, r_hi], axis=1)
    gate2 = mod_ref[0, :, 5 * d:6 * d]
    out = _ln(alpha * x1_ref[...] + gate2 * m) * lnw_ref[...] + lnb_ref[...]

    def store(o_ref):
        o_ref[...] = out

    rows.when_prompt(i, lambda: store(op_ref))
    rows.when_latent(i, lambda: store(os_ref))


def _combine(pos, wts, hu, x1, mod3, ln_w, ln_b, wg, wu, wd, ybuf, rows, n_prompt_rows, alpha):
    t, d = x1.shape
    tc = rows.tile
    const = lambda a: pl.BlockSpec(a.shape, lambda i: (0, 0))
    return pl.pallas_call(
        functools.partial(_combine_kernel, d=d, alpha=alpha, rows=rows),
        grid=(rows.n_tiles,),
        in_specs=[
            pl.BlockSpec((TOP_K, tc), lambda i: (0, i), memory_space=pltpu.SMEM),
            pl.BlockSpec((tc, TOP_K), lambda i: (i, 0)),
            pl.BlockSpec((tc, d // 2), lambda i: (i, 0)),
            pl.BlockSpec((tc, d), lambda i: (i, 0)),
            pl.BlockSpec((1, 1, mod3.shape[2]), lambda i: (rows.mod_row(i), 0, 0)),
            const(ln_w), const(ln_b), const(wg), const(wu), const(wd),
            pl.BlockSpec(memory_space=pl.ANY),
        ],
        out_specs=[pl.BlockSpec((tc, d), lambda i: (rows.prompt_tile(i), 0)),
                   pl.BlockSpec((tc, d), lambda i: (rows.latent_tile(i), 0))],
        out_shape=[jax.ShapeDtypeStruct((n_prompt_rows, d), F32),
                   jax.ShapeDtypeStruct((t - n_prompt_rows, d), F32)],
        scratch_shapes=[pltpu.VMEM((TOP_K * tc, 1, d // 2), U32), pltpu.VMEM((TOP_K * tc, d // 2), U32),
                        pltpu.SemaphoreType.DMA],
        compiler_params=_params("arbitrary"),
        name="combine",
    )(pos, wts, hu, x1, mod3, ln_w, ln_b, wg, wu, wd, ybuf)


def kernel(x_prompt, x_sample, c, state_ret, c_ctx, w_ada, b_ada, w_in, ret_decay, ret_gn_w, w_out,
           ln1_w, ln1_b, w_router, router_bias, w_gate_e, w_up_e, w_down_e, w_gate_s, w_up_s, w_down_s,
           ln2_w, ln2_b):
    batch, seq, d = x_prompt.shape
    dec_batch, dec_seq, _ = x_sample.shape
    depth = w_in.shape[0]
    assert depth == 1
    n_prompt_rows = batch * seq
    n_latent_rows = dec_batch * dec_seq
    t = n_prompt_rows + n_latent_rows
    alpha = (2.0 * depth) ** 0.25
    d_ret = H_RET * CHUNK
    fg = (w_in.shape[2] - 4 * d_ret) // N_FGROUPS
    l = 0
    tiling = lambda tile: _Rows(tile, n_prompt_rows, n_latent_rows, dec_seq)

    xp = x_prompt.reshape(n_prompt_rows, d)
    xs = x_sample.reshape(n_latent_rows, d)

    n_mod = 1 + dec_batch
    cc = jnp.concatenate([c_ctx[None, :], c, jnp.zeros((-n_mod % 8, d), F32)], axis=0)
    mod = _ada(cc, w_ada[l], b_ada[l][None, :])
    mod3 = mod.reshape(mod.shape[0], 1, mod.shape[1])

    z = _inproj(xp, xs, mod3, w_in[l], tiling(ROW_TILE))

    log_g = jax.nn.log_sigmoid(ret_decay[l].astype(F32))
    gn_w = ret_gn_w[l][None, :]
    latent_row_block = n_prompt_rows // dec_seq
    ret_p, ctx_state = _retention(z, log_g, gn_w, batch=batch, seq=seq, row_block_offset=0, latent=False)
    ret_s = _retention(z, log_g, gn_w, batch=dec_batch, seq=dec_seq, row_block_offset=latent_row_block,
                       latent=True, state=state_ret[:, l])[0]
    fcol = 4 * d_ret // fg
    four_p = _fourier(z, batch=batch, seq=seq, row_block_offset=0, col_block_offset=fcol, fg=fg)
    four_s = _fourier(z, batch=dec_batch, seq=dec_seq, row_block_offset=latent_row_block,
                      col_block_offset=fcol, fg=fg)

    x1, hu, logits_t = _outproj(ret_p, ret_s, four_p, four_s, _cast_bf16(w_out[l]), xp, xs, mod3,
                                ln1_w[l][None, :], ln1_b[l][None, :], w_router[l].T,
                                tiling(OUT_ROW_TILE), alpha)

    eidx, wts, rank, counts = _router(logits_t, router_bias[l][:, None])
    meta, start = _expert_schedule(counts[:, 0].astype(I32), t * TOP_K)
    pos = _positions(start, eidx, rank)

    xsorted = _dispatch(pos, hu)
    ybuf = _experts(meta, xsorted, w_gate_e[l], w_up_e[l], w_down_e[l])
    y_prompt, y_sample = _combine(pos, wts.T, hu, x1, mod3, ln2_w[l][None, :], ln2_b[l][None, :],
                                  _cast_bf16(w_gate_s[l]), _cast_bf16(w_up_s[l]), _cast_bf16(w_down_s[l]),
                                  ybuf, tiling(COMBINE_TILE), n_prompt_rows, alpha)

    new_state = ctx_state[:, None].astype(x_prompt.dtype)
    return (y_prompt.reshape(batch, seq, d), y_sample.reshape(dec_batch, dec_seq, d), new_state)
```

```python
import functools

import numpy as np
import jax
import jax.numpy as jnp
from jax import lax
from jax.experimental import pallas as pl
from jax.experimental.pallas import tpu as pltpu

F32 = jnp.float32
BF16 = jnp.bfloat16
I32 = jnp.int32
U32 = jnp.uint32

GRID_W = 64
H_RET = 8
N_FGROUPS = 4
CHUNK = 128
ROPE_THETA = 10000.0
N_EXPERTS = 64
TOP_K = 8
N_GROUPS = 8
TOPK_GROUP = 4
ROUTED_SCALE = 2.5
LN_EPS = 1e-6
GN_EPS = 1e-5
N_DMA_PRIORITIES = 2

VMEM_LIMIT_BYTES = 56 * 1024 * 1024

ROW_TILE = 1024
IN_COL_TILE = 512
OUT_ROW_TILE = 256
ADA_COL_TILE = 1024
ROUTER_TILE = 512
DISPATCH_TILE = 256
EXPERT_ROWS = 256
COMBINE_TILE = 256
LN_ROWS = 256
RET_HEADS_PER_STEP = 4
TOKENS_PER_ISSUE_ITER = 4


def _params(*sem):
    return pltpu.CompilerParams(dimension_semantics=sem, vmem_limit_bytes=VMEM_LIMIT_BYTES)


def _ln(x):
    mu = jnp.mean(x, axis=-1, keepdims=True)
    xc = x - mu
    var = jnp.mean(xc * xc, axis=-1, keepdims=True)
    return xc * lax.rsqrt(var + LN_EPS)


def _silu(x):
    return x * jax.nn.sigmoid(x)


def _dot(a, b):
    return jnp.dot(a, b, preferred_element_type=F32)


def _dot_nt(a, b):
    return lax.dot_general(a, b, (((1,), (1,)), ((), ())), preferred_element_type=F32)


def _dot_tn(a, b):
    return lax.dot_general(a, b, (((0,), (0,)), ((), ())), preferred_element_type=F32)


def _pack_halves(x):
    n = x.shape[1] // 2
    lo = lax.bitcast_convert_type(x[:, :n].astype(BF16).astype(F32), U32)
    hi = lax.bitcast_convert_type(x[:, n:].astype(BF16).astype(F32), U32)
    return (lo >> 16) | hi


def _unpack_halves(u):
    lo = lax.bitcast_convert_type(u << 16, F32)
    hi = lax.bitcast_convert_type(u & jnp.uint32(0xFFFF0000), F32)
    return lo, hi


def _dot_halves(lo, hi, w_ref):
    n = lo.shape[1]
    return _dot(lo, w_ref[0:n, :]) + _dot(hi, w_ref[n:2 * n, :])


def _ada_kernel(c_ref, w_ref, b_ref, o_ref):
    s = _silu(c_ref[...]).astype(BF16)
    o_ref[...] = _dot(s, w_ref[...].astype(BF16)) + b_ref[...]


def _ada(cc, w_ada, b_ada):
    rows, d = cc.shape
    n = w_ada.shape[1]
    return pl.pallas_call(
        _ada_kernel,
        grid=(n // ADA_COL_TILE,),
        in_specs=[
            pl.BlockSpec((rows, d), lambda j: (0, 0)),
            pl.BlockSpec((d, ADA_COL_TILE), lambda j: (0, j)),
            pl.BlockSpec((1, ADA_COL_TILE), lambda j: (0, j)),
        ],
        out_specs=pl.BlockSpec((rows, ADA_COL_TILE), lambda j: (0, j)),
        out_shape=jax.ShapeDtypeStruct((rows, n), F32),
        compiler_params=_params("arbitrary"),
        name="ada",
    )(cc, w_ada, b_ada)


class _Rows:
    def __init__(self, tile, n_prompt_rows, n_latent_rows, dec_seq):
        assert n_prompt_rows % tile == 0 and n_latent_rows % tile == 0 and dec_seq % tile == 0
        self.tile = tile
        self.n_prompt_tiles = n_prompt_rows // tile
        self.n_tiles = (n_prompt_rows + n_latent_rows) // tile
        self.tiles_per_latent_batch = dec_seq // tile

    def prompt_tile(self, i):
        return jnp.minimum(i, self.n_prompt_tiles - 1)

    def latent_tile(self, i):
        return jnp.maximum(i - self.n_prompt_tiles, 0)

    def mod_row(self, i):
        return jnp.where(i < self.n_prompt_tiles, 0,
                         1 + (i - self.n_prompt_tiles) // self.tiles_per_latent_batch)

    def when_prompt(self, i, fn, extra=True):
        pl.when(jnp.logical_and(extra, i < self.n_prompt_tiles))(fn)

    def when_latent(self, i, fn, extra=True):
        pl.when(jnp.logical_and(extra, i >= self.n_prompt_tiles))(fn)


def _inproj_kernel(xp_ref, xs_ref, mod_ref, w_ref, z_ref, h_scr, *, d, rows):
    i = pl.program_id(0)
    first = pl.program_id(1) == 0

    def fill(x_ref):
        shift = mod_ref[0, :, 0:d]
        scale = mod_ref[0, :, d:2 * d]
        for r in range(0, x_ref.shape[0], LN_ROWS):
            rs = slice(r, r + LN_ROWS)
            h_scr[rs, :] = (_ln(x_ref[rs, :]) * (1.0 + scale) + shift).astype(BF16)

    rows.when_prompt(i, lambda: fill(xp_ref), first)
    rows.when_latent(i, lambda: fill(xs_ref), first)
    z_ref[...] = _dot(h_scr[...], w_ref[...].astype(BF16)).astype(BF16)


def _inproj(xp, xs, mod3, w_in, rows):
    d = xp.shape[1]
    n = w_in.shape[1]
    tm = rows.tile
    return pl.pallas_call(
        functools.partial(_inproj_kernel, d=d, rows=rows),
        grid=(rows.n_tiles, n // IN_COL_TILE),
        in_specs=[
            pl.BlockSpec((tm, d), lambda i, j: (rows.prompt_tile(i), 0)),
            pl.BlockSpec((tm, d), lambda i, j: (rows.latent_tile(i), 0)),
            pl.BlockSpec((1, 1, mod3.shape[2]), lambda i, j: (rows.mod_row(i), 0, 0)),
            pl.BlockSpec((d, IN_COL_TILE), lambda i, j: (0, j)),
        ],
        out_specs=pl.BlockSpec((tm, IN_COL_TILE), lambda i, j: (i, j)),
        out_shape=jax.ShapeDtypeStruct((rows.n_tiles * tm, n), BF16),
        scratch_shapes=[pltpu.VMEM((tm, d), BF16)],
        compiler_params=_params("arbitrary", "arbitrary"),
        name="inproj",
    )(xp, xs, mod3, w_in)


def _rope_tables(seq):
    dh = CHUNK
    nf = dh // 4
    pos = np.arange(seq)
    row = (pos // GRID_W).astype(np.float64)
    col = (pos % GRID_W).astype(np.float64)
    freqs = ROPE_THETA ** (-np.arange(nf, dtype=np.float64) / nf)
    ar = row[:, None] * freqs[None, :]
    ac = col[:, None] * freqs[None, :]
    cos = np.concatenate([np.cos(ar), np.cos(ar), np.cos(ac), np.cos(ac)], axis=1)
    sin = np.concatenate([-np.sin(ar), np.sin(ar), -np.sin(ac), np.sin(ac)], axis=1)
    return jnp.asarray(cos, F32), jnp.asarray(sin, F32)


def _retention_kernel(*refs, seq, latent, k_scale):
    if latent:
        (lg_ref, q_ref, k_ref, v_ref, g_ref, gnw_ref, cos_ref, sin_ref, s0_ref,
         o_ref, q_scr, k_scr, o_scr) = refs
        st_ref = None
    else:
        (lg_ref, q_ref, k_ref, v_ref, g_ref, gnw_ref,
         o_ref, st_ref, q_scr, k_scr, o_scr) = refs
        s0_ref = cos_ref = sin_ref = None
    c = CHUNK
    n = seq // c
    heads = range(RET_HEADS_PER_STEP)
    h0 = pl.program_id(1) * RET_HEADS_PER_STEP
    cols = lambda hh: slice(hh * c, (hh + 1) * c)

    ii = lax.broadcasted_iota(I32, (c, c), 0).astype(F32)
    jj = lax.broadcasted_iota(I32, (c, c), 1).astype(F32)
    diff = ii - jj
    neg_inf = jnp.float32(-jnp.inf)
    idx = lax.broadcasted_iota(I32, (c, 1), 0).astype(F32)
    ones_row = jnp.ones((1, c), F32)

    def decays(hh):
        lgf = lg_ref[0, h0 + hh]
        lgb = lg_ref[1, h0 + hh]
        return dict(
            dmask=(jnp.exp(jnp.where(diff >= 0, diff * lgf, neg_inf))
                   + jnp.exp(jnp.where(diff <= 0, (-diff) * lgb, neg_inf))),
            qdec_f=jnp.exp((idx + 1.0) * lgf), kdec_f=jnp.exp((c - 1.0 - idx) * lgf),
            qdec_b=jnp.exp((c - idx) * lgb), kdec_b=jnp.exp(idx * lgb),
            cdec_f=jnp.exp(ones_row * (c * lgf)), cdec_b=jnp.exp(ones_row * (c * lgb)))

    dec = [decays(hh) for hh in heads]

    if latent:
        lane = lax.broadcasted_iota(I32, (c, c), 1)
        first_quarter = (lane % (c // 2)) < (c // 4)

        def rope(x, rows):
            partner = jnp.where(first_quarter, pltpu.roll(x, c - c // 4, 1), pltpu.roll(x, c // 4, 1))
            return x * cos_ref[rows, :] + partner * sin_ref[rows, :]

    for ci in range(n):
        rows = slice(ci * c, (ci + 1) * c)
        for hh in heads:
            q = q_ref[rows, cols(hh)].astype(F32)
            k = k_ref[rows, cols(hh)].astype(F32) * k_scale
            if latent:
                q = rope(q, rows)
                k = rope(k, rows)
            q_scr[rows, cols(hh)] = q
            k_scr[rows, cols(hh)] = k
            scores = _dot_nt(q.astype(BF16), k.astype(BF16)) * dec[hh]["dmask"]
            o_scr[rows, cols(hh)] = _dot(scores.astype(BF16), v_ref[rows, cols(hh)])

    s = [s0_ref[0, hh] if latent else jnp.zeros((c, c), F32) for hh in heads]
    for ci in range(n):
        rows = slice(ci * c, (ci + 1) * c)
        for hh in heads:
            q = q_scr[rows, cols(hh)]
            k = k_scr[rows, cols(hh)]
            o_scr[rows, cols(hh)] += _dot((q * dec[hh]["qdec_f"]).astype(BF16), s[hh].astype(BF16))
            s[hh] = s[hh] * dec[hh]["cdec_f"] + _dot_tn((k * dec[hh]["kdec_f"]).astype(BF16),
                                                          v_ref[rows, cols(hh)])
    if st_ref is not None:
        for hh in heads:
            st_ref[0, hh] = s[hh]

    s = [s0_ref[1, hh] if latent else jnp.zeros((c, c), F32) for hh in heads]
    for ci in reversed(range(n)):
        rows = slice(ci * c, (ci + 1) * c)
        for hh in heads:
            q = q_scr[rows, cols(hh)]
            k = k_scr[rows, cols(hh)]
            o = o_scr[rows, cols(hh)] + _dot((q * dec[hh]["qdec_b"]).astype(BF16), s[hh].astype(BF16))
            s[hh] = s[hh] * dec[hh]["cdec_b"] + _dot_tn((k * dec[hh]["kdec_b"]).astype(BF16),
                                                          v_ref[rows, cols(hh)])
            mu = jnp.mean(o, axis=-1, keepdims=True)
            oc = o - mu
            var = jnp.mean(oc * oc, axis=-1, keepdims=True)
            on = oc * lax.rsqrt(var + GN_EPS) * gnw_ref[:, cols(hh)]
            o_ref[rows, cols(hh)] = (_silu(g_ref[rows, cols(hh)].astype(F32)) * on).astype(BF16)
    if st_ref is not None:
        for hh in heads:
            st_ref[1, hh] = s[hh]


def _retention(z, log_g, gn_w, *, batch, seq, row_block_offset, latent, state=None):
    dh = CHUNK
    hp = RET_HEADS_PER_STEP
    d_ret = H_RET * dh
    zspec = lambda part: pl.BlockSpec((seq, hp * dh),
                                      lambda b, h: (row_block_offset + b, part * (H_RET // hp) + h))
    in_specs = [
        pl.BlockSpec(memory_space=pltpu.SMEM),
        zspec(0), zspec(1), zspec(2), zspec(3),
        pl.BlockSpec((1, hp * dh), lambda b, h: (0, h)),
    ]
    args = [log_g, z, z, z, z, gn_w]
    out_specs = [pl.BlockSpec((seq, hp * dh), lambda b, h: (b, h))]
    out_shape = [jax.ShapeDtypeStruct((batch * seq, d_ret), BF16)]
    state_spec = pl.BlockSpec((None, 2, hp, dh, dh), lambda b, h: (b, 0, h, 0, 0))
    if latent:
        cos, sin = _rope_tables(seq)
        in_specs += [pl.BlockSpec((seq, dh), lambda b, h: (0, 0)),
                     pl.BlockSpec((seq, dh), lambda b, h: (0, 0)),
                     state_spec]
        args += [cos, sin, state]
    else:
        out_specs.append(state_spec)
        out_shape.append(jax.ShapeDtypeStruct((batch, 2, H_RET, dh, dh), F32))
    return pl.pallas_call(
        functools.partial(_retention_kernel, seq=seq, latent=latent, k_scale=dh ** -0.5),
        grid=(batch, H_RET // hp),
        in_specs=in_specs,
        out_specs=out_specs,
        out_shape=out_shape,
        scratch_shapes=[pltpu.VMEM((seq, hp * dh), F32)] * 3,
        compiler_params=_params("arbitrary", "arbitrary"),
        name="retention_latent" if latent else "retention_context",
    )(*args)


def _dft_tables(n):
    jk = np.outer(np.arange(n), np.arange(n)) % n
    ang = 2.0 * np.pi * jk / n
    return np.cos(ang) / np.sqrt(n), np.sin(ang) / np.sqrt(n)


def _fourier_kernel(x_ref, chan_ref, posn_ref, o_ref, *, fg):
    for g in range(N_FGROUPS):
        cols = slice(g * fg, (g + 1) * fg)
        ab = _dot(x_ref[:, cols], chan_ref[...]).astype(BF16)
        stacked = jnp.concatenate([ab[:, :fg], ab[:, fg:]], axis=0)
        o_ref[:, cols] = _dot(posn_ref[...], stacked).astype(BF16)


def _fourier(z, *, batch, seq, row_block_offset, col_block_offset, fg):
    cl, sl = _dft_tables(seq)
    cc, sc = _dft_tables(fg)
    chan = jnp.asarray(np.concatenate([cc, sc], axis=1), BF16)
    posn = jnp.asarray(np.concatenate([cl, -sl], axis=1), BF16)
    width = N_FGROUPS * fg
    assert (col_block_offset * fg) % width == 0
    return pl.pallas_call(
        functools.partial(_fourier_kernel, fg=fg),
        grid=(batch,),
        in_specs=[pl.BlockSpec((seq, width), lambda b: (row_block_offset + b, col_block_offset * fg // width)),
                  pl.BlockSpec(chan.shape, lambda b: (0, 0)),
                  pl.BlockSpec(posn.shape, lambda b: (0, 0))],
        out_specs=pl.BlockSpec((seq, width), lambda b: (b, 0)),
        out_shape=jax.ShapeDtypeStruct((batch * seq, width), BF16),
        compiler_params=_params("arbitrary"),
        name=f"fourier_{seq}",
    )(z, chan, posn)


def _outproj_kernel(retp_ref, rets_ref, fourp_ref, fours_ref, w_ref, xp_ref, xs_ref, mod_ref,
                    lnw_ref, lnb_ref, wr_ref, x1_ref, hu_ref, lg_ref, *, d, d_ret, alpha, rows):
    i = pl.program_id(0)

    def body(ret_ref, four_ref, x_ref):
        gate1 = mod_ref[0, :, 2 * d:3 * d]
        shift2 = mod_ref[0, :, 3 * d:4 * d]
        scale2 = mod_ref[0, :, 4 * d:5 * d]
        mix = _dot(ret_ref[...], w_ref[0:d_ret, :]) + _dot(four_ref[...], w_ref[d_ret:, :])
        x1 = _ln(alpha * x_ref[...] + gate1 * mix) * lnw_ref[...] + lnb_ref[...]
        x1_ref[...] = x1
        h2 = _ln(x1) * (1.0 + scale2) + shift2
        hu_ref[...] = _pack_halves(h2)
        wr = wr_ref[...]
        w_hi = wr.astype(BF16)
        w_lo = (wr - w_hi.astype(F32)).astype(BF16)
        h_hi = h2.astype(BF16)
        h_lo = (h2 - h_hi.astype(F32)).astype(BF16)
        lg_ref[...] = _dot_nt(w_hi, h_hi) + (_dot_nt(w_hi, h_lo) + _dot_nt(w_lo, h_hi))

    rows.when_prompt(i, lambda: body(retp_ref, fourp_ref, xp_ref))
    rows.when_latent(i, lambda: body(rets_ref, fours_ref, xs_ref))


def _outproj(retp, rets, fourp, fours, w_out_bf16, xp, xs, mod3, ln_w, ln_b, w_router_t, rows, alpha):
    d = xp.shape[1]
    d_ret = retp.shape[1]
    d_four = fourp.shape[1]
    tm = rows.tile
    t = rows.n_tiles * tm
    n_e = w_router_t.shape[0]
    prompt = lambda w: pl.BlockSpec((tm, w), lambda i: (rows.prompt_tile(i), 0))
    latent = lambda w: pl.BlockSpec((tm, w), lambda i: (rows.latent_tile(i), 0))
    const = lambda a: pl.BlockSpec(a.shape, lambda i: (0, 0))
    return pl.pallas_call(
        functools.partial(_outproj_kernel, d=d, d_ret=d_ret, alpha=alpha, rows=rows),
        grid=(rows.n_tiles,),
        in_specs=[
            prompt(d_ret), latent(d_ret), prompt(d_four), latent(d_four),
            const(w_out_bf16),
            prompt(d), latent(d),
            pl.BlockSpec((1, 1, mod3.shape[2]), lambda i: (rows.mod_row(i), 0, 0)),
            const(ln_w), const(ln_b), const(w_router_t),
        ],
        out_specs=[
            pl.BlockSpec((tm, d), lambda i: (i, 0)),
            pl.BlockSpec((tm, d // 2), lambda i: (i, 0)),
            pl.BlockSpec((n_e, tm), lambda i: (0, i)),
        ],
        out_shape=[
            jax.ShapeDtypeStruct((t, d), F32),
            jax.ShapeDtypeStruct((t, d // 2), U32),
            jax.ShapeDtypeStruct((n_e, t), F32),
        ],
        compiler_params=_params("arbitrary"),
        name="outproj",
    )(retp, rets, fourp, fours, w_out_bf16, xp, xs, mod3, ln_w, ln_b, w_router_t)


def _first_index_of_max(vals, index, sentinel, axis):
    m = jnp.max(vals, axis=axis, keepdims=True)
    return jnp.min(jnp.where(vals == m, index, sentinel), axis=axis, keepdims=True), m


def _router_kernel(lg_ref, bias_ref, eidx_ref, w_ref, rank_ref, cnt_ref, carry):
    i = pl.program_id(0)
    tr = lg_ref.shape[1]
    gsz = N_EXPERTS // N_GROUPS
    neg_inf = jnp.float32(-jnp.inf)

    @pl.when(i == 0)
    def _():
        carry[...] = jnp.zeros_like(carry)

    scores = jax.nn.sigmoid(lg_ref[...])
    biased = scores + bias_ref[...]
    b3 = biased.reshape(N_GROUPS, gsz, tr)
    sub = lax.broadcasted_iota(I32, (N_GROUPS, gsz, tr), 1)
    first, m1 = _first_index_of_max(b3, sub, gsz, 1)
    m2 = jnp.max(jnp.where(sub == first, neg_inf, b3), axis=1, keepdims=True)
    gscore = (m1 + m2).reshape(N_GROUPS, tr)

    gi = lax.broadcasted_iota(I32, (N_GROUPS, tr), 0)
    gsel = jnp.zeros((N_GROUPS, tr), jnp.bool_)
    cur = gscore
    for _ in range(TOPK_GROUP):
        first, _m = _first_index_of_max(cur, gi, N_GROUPS, 0)
        pick = gi == first
        gsel = jnp.logical_or(gsel, pick)
        cur = jnp.where(pick, neg_inf, cur)

    gsel3 = jnp.broadcast_to(gsel.reshape(N_GROUPS, 1, tr), (N_GROUPS, gsz, tr))
    cur = jnp.where(gsel3, b3, neg_inf).reshape(N_EXPERTS, tr)
    ei = lax.broadcasted_iota(I32, (N_EXPERTS, tr), 0)
    picks = []
    sel_scores = []
    for _ in range(TOP_K):
        first, _m = _first_index_of_max(cur, ei, N_EXPERTS, 0)
        pick = ei == first
        picks.append(first)
        sel_scores.append(jnp.sum(jnp.where(pick, scores, 0.0), axis=0, keepdims=True))
        cur = jnp.where(pick, neg_inf, cur)
    total = sel_scores[0]
    for sc in sel_scores[1:]:
        total = total + sc

    member = jnp.zeros((N_EXPERTS, tr), F32)
    for first in picks:
        member = member + jnp.where(ei == first, 1.0, 0.0)
    src = lax.broadcasted_iota(I32, (tr, tr), 0)
    dst = lax.broadcasted_iota(I32, (tr, tr), 1)
    upper = jnp.where(src < dst, 1.0, 0.0).astype(BF16)
    rank_all = _dot(member.astype(BF16), upper) + carry[...]
    for k in range(TOP_K):
        pick = ei == picks[k]
        eidx_ref[k:k + 1, :] = picks[k]
        w_ref[k:k + 1, :] = sel_scores[k] / total * ROUTED_SCALE
        rank_ref[k:k + 1, :] = jnp.sum(jnp.where(pick, rank_all, 0.0), axis=0, keepdims=True).astype(I32)
    carry[...] = carry[...] + jnp.sum(member, axis=1, keepdims=True)
    cnt_ref[...] = carry[...]


def _router(logits_t, bias):
    n_e, t = logits_t.shape
    tr = ROUTER_TILE
    tok = pl.BlockSpec((TOP_K, tr), lambda i: (0, i))
    return pl.pallas_call(
        _router_kernel,
        grid=(t // tr,),
        in_specs=[pl.BlockSpec((n_e, tr), lambda i: (0, i)),
                  pl.BlockSpec((n_e, 1), lambda i: (0, 0))],
        out_specs=[tok, tok, tok, pl.BlockSpec((n_e, 1), lambda i: (0, 0))],
        out_shape=[jax.ShapeDtypeStruct((TOP_K, t), I32),
                   jax.ShapeDtypeStruct((TOP_K, t), F32),
                   jax.ShapeDtypeStruct((TOP_K, t), I32),
                   jax.ShapeDtypeStruct((n_e, 1), F32)],
        scratch_shapes=[pltpu.VMEM((n_e, 1), F32)],
        compiler_params=_params("arbitrary"),
        name="router",
    )(logits_t, bias)


def _positions_kernel(start_ref, eidx_ref, rank_ref, pos_ref):
    e = eidx_ref[...]
    pos = rank_ref[...]
    for x in range(N_EXPERTS):
        pos = pos + jnp.where(e == x, start_ref[x], 0)
    pos_ref[...] = pos


def _positions(start, eidx, rank):
    full = pl.BlockSpec(eidx.shape, lambda i: (0, 0))
    return pl.pallas_call(
        _positions_kernel,
        grid=(1,),
        in_specs=[pl.BlockSpec(memory_space=pltpu.SMEM), full, full],
        out_specs=full,
        out_shape=jax.ShapeDtypeStruct(eidx.shape, I32),
        compiler_params=_params("arbitrary"),
        name="positions",
    )(start, eidx, rank)


def _scatter_row(src_ref, src_row, dst_ref, dst_row, sem):
    return pltpu.make_async_copy(src_ref.at[pl.ds(src_row, 1)], dst_ref.at[dst_row], sem)


def _gather_row(src_ref, src_row, dst_ref, dst_row, sem):
    return pltpu.make_async_copy(src_ref.at[src_row], dst_ref.at[dst_row], sem)


def _dispatch_kernel(pos_ref, h_ref, xs_ref, sem):
    td = h_ref.shape[0]

    def issue(it, carry):
        for u in range(TOKENS_PER_ISSUE_ITER):
            t = it * TOKENS_PER_ISSUE_ITER + u
            for k in range(TOP_K):
                _scatter_row(h_ref, t, xs_ref, pos_ref[k, t], sem).start(priority=k % N_DMA_PRIORITIES)
        return carry

    lax.fori_loop(0, td // TOKENS_PER_ISSUE_ITER, issue, 0)

    def drain(t, carry):
        for k in range(TOP_K):
            _scatter_row(h_ref, 0, xs_ref, 0, sem).wait()
        return carry

    lax.fori_loop(0, td, drain, 0)


def _dispatch(pos, hu):
    t, w = hu.shape
    td = DISPATCH_TILE
    return pl.pallas_call(
        _dispatch_kernel,
        grid=(t // td,),
        in_specs=[pl.BlockSpec((TOP_K, td), lambda i: (0, i), memory_space=pltpu.SMEM),
                  pl.BlockSpec((td, w), lambda i: (i, 0))],
        out_specs=pl.BlockSpec(memory_space=pl.ANY),
        out_shape=jax.ShapeDtypeStruct((t * TOP_K, 1, w), hu.dtype),
        scratch_shapes=[pltpu.SemaphoreType.DMA],
        compiler_params=_params("arbitrary"),
        name="dispatch",
    )(pos, hu)


def _experts_kernel(blk_ref, exp_ref, lo_ref, hi_ref, np_ref, slot_ref, nxt_ref, x_ref,
                    wg_hbm, wu_hbm, wd_hbm, y_ref, wg_f, wu_f, wd_f, wg_s, wu_s, wd_s, x2_s, y2_s, sems):
    s = pl.program_id(0)
    prev = jnp.maximum(s - 1, 0)
    valid = s < np_ref[0]
    new_expert = jnp.logical_or(s == 0, exp_ref[s] != exp_ref[prev])
    first_visit = jnp.logical_or(s == 0, blk_ref[s] != blk_ref[prev])
    tm = x_ref.shape[0]

    def weight_copies(e, slot):
        return [pltpu.make_async_copy(hbm.at[e], buf.at[slot], sems.at[slot, m])
                for m, (hbm, buf) in enumerate(((wg_hbm, wg_f), (wu_hbm, wu_f), (wd_hbm, wd_f)))]

    @pl.when(s == 0)
    def _():
        for cp in weight_copies(exp_ref[0], slot_ref[0]):
            cp.start()

    @pl.when(jnp.logical_and(valid, new_expert))
    def _():
        slot = slot_ref[s]
        for cp in weight_copies(exp_ref[s], slot):
            cp.wait()

        @pl.when(nxt_ref[s] >= 0)
        def _():
            for cp in weight_copies(nxt_ref[s], 1 - slot):
                cp.start()

        wg_s[...] = wg_f[slot].astype(BF16)
        wu_s[...] = wu_f[slot].astype(BF16)
        wd_s[...] = wd_f[slot].astype(BF16)

    def compute():
        x2_s[...] = x_ref[...].reshape(x2_s.shape)
        lo, hi = _unpack_halves(x2_s[...])
        lo = lo.astype(BF16)
        hi = hi.astype(BF16)
        a = (_silu(_dot_halves(lo, hi, wg_s)) * _dot_halves(lo, hi, wu_s)).astype(BF16)
        y = _pack_halves(_dot(a, wd_s[...]))
        rows = blk_ref[s] * tm + lax.broadcasted_iota(I32, (tm, 1), 0)
        mine = jnp.logical_and(rows >= lo_ref[s], rows < hi_ref[s])
        return y, mine

    @pl.when(jnp.logical_and(valid, first_visit))
    def _():
        y, mine = compute()
        y2_s[...] = jnp.where(mine, y, jnp.uint32(0))
        y_ref[...] = y2_s[...].reshape(y_ref.shape)

    @pl.when(jnp.logical_and(valid, jnp.logical_not(first_visit)))
    def _():
        y, mine = compute()
        y2_s[...] = jnp.where(mine, y, y2_s[...])
        y_ref[...] = y2_s[...].reshape(y_ref.shape)


def _experts(meta, xs, w_gate_e, w_up_e, w_down_e):
    n_rows, _, w = xs.shape
    d = w_gate_e.shape[1]
    d_e = w_gate_e.shape[2]
    tm = EXPERT_ROWS
    n_steps = meta[0].shape[0]
    rows_of_step = lambda s, blk, *_: (blk[s], 0, 0)
    hbm = pl.BlockSpec(memory_space=pl.ANY)
    grid_spec = pltpu.PrefetchScalarGridSpec(
        num_scalar_prefetch=len(meta),
        grid=(n_steps,),
        in_specs=[pl.BlockSpec((tm, 1, w), rows_of_step), hbm, hbm, hbm],
        out_specs=pl.BlockSpec((tm, 1, w), rows_of_step),
        scratch_shapes=[pltpu.VMEM((2, d, d_e), F32), pltpu.VMEM((2, d, d_e), F32), pltpu.VMEM((2, d_e, d), F32),
                        pltpu.VMEM((d, d_e), BF16), pltpu.VMEM((d, d_e), BF16), pltpu.VMEM((d_e, d), BF16),
                        pltpu.VMEM((tm, w), U32), pltpu.VMEM((tm, w), U32),
                        pltpu.SemaphoreType.DMA((2, 3))],
    )
    return pl.pallas_call(
        _experts_kernel,
        grid_spec=grid_spec,
        out_shape=jax.ShapeDtypeStruct((n_rows, 1, w), xs.dtype),
        compiler_params=_params("arbitrary"),
        name="experts",
    )(*meta, xs, w_gate_e, w_up_e, w_down_e)


def _lookup(table, idx):
    hit = idx[:, None] == jnp.arange(table.shape[0], dtype=I32)[None, :]
    return jnp.sum(jnp.where(hit, table[None, :], 0), axis=1)


def _expert_schedule(counts, n_rows):
    tm = EXPERT_ROWS
    n_blocks = n_rows // tm
    n_steps = n_blocks + N_EXPERTS - 1
    end = jnp.cumsum(counts)
    start = end - counts
    first_blk = start // tm
    last_blk = (end - 1) // tm
    n_blk = jnp.where(counts > 0, last_blk - first_blk + 1, 0)
    pair_end = jnp.cumsum(n_blk)
    pair_start = pair_end - n_blk
    n_pairs = pair_end[-1]
    step = jnp.minimum(jnp.arange(n_steps, dtype=I32), n_pairs - 1)
    exp = jnp.sum((pair_end[None, :] <= step[:, None]).astype(I32), axis=1)
    blk = _lookup(first_blk - pair_start, exp) + step
    active = n_blk > 0
    order = jnp.cumsum(active.astype(I32)) - 1
    ids = jnp.arange(N_EXPERTS, dtype=I32)
    later = jnp.where(jnp.logical_and(active[None, :], ids[None, :] > ids[:, None]), ids[None, :], N_EXPERTS)
    following = jnp.min(later, axis=1)
    following = jnp.where(following == N_EXPERTS, -1, following)
    meta = (blk, exp, _lookup(start, exp), _lookup(end, exp), n_pairs.reshape(1),
            _lookup(order % 2, exp), _lookup(following, exp))
    return meta, start


def _cast_kernel(x_ref, o_ref):
    o_ref[...] = x_ref[...].astype(o_ref.dtype)


def _cast_bf16(w):
    r, c = w.shape
    tile = min(r, 512)
    return pl.pallas_call(
        _cast_kernel,
        grid=(r // tile,),
        in_specs=[pl.BlockSpec((tile, c), lambda i: (i, 0))],
        out_specs=pl.BlockSpec((tile, c), lambda i: (i, 0)),
        out_shape=jax.ShapeDtypeStruct((r, c), BF16),
        compiler_params=_params("arbitrary"),
        name="cast_bf16",
    )(w)


def _combine_kernel(pos_ref, w_ref, hu_ref, x1_ref, mod_ref, lnw_ref, lnb_ref, wg_ref, wu_ref, wd_ref,
                    y_ref, op_ref, os_ref, gbuf, g2_s, sem, *, d, alpha, rows):
    i = pl.program_id(0)
    tc = hu_ref.shape[0]

    def issue(it, carry):
        for u in range(TOKENS_PER_ISSUE_ITER):
            t = it * TOKENS_PER_ISSUE_ITER + u
            for k in range(TOP_K):
                _gather_row(y_ref, pos_ref[k, t], gbuf, k * tc + t, sem).start(priority=k % N_DMA_PRIORITIES)
        return carry

    lax.fori_loop(0, tc // TOKENS_PER_ISSUE_ITER, issue, 0)

    lo, hi = _unpack_halves(hu_ref[...])
    lo = lo.astype(BF16)
    hi = hi.astype(BF16)
    a = (_silu(_dot_halves(lo, hi, wg_ref)) * _dot_halves(lo, hi, wu_ref)).astype(BF16)
    shared = _dot(a, wd_ref[...])

    def drain(t, carry):
        for k in range(TOP_K):
            _gather_row(y_ref, 0, gbuf, 0, sem).wait()
        return carry

    lax.fori_loop(0, tc, drain, 0)

    g2_s[...] = gbuf[...].reshape(g2_s.shape)
    w = w_ref[...]
    r_lo = jnp.zeros((tc, d // 2), F32)
    r_hi = jnp.zeros((tc, d // 2), F32)
    for k in range(TOP_K):
        lo, hi = _unpack_halves(g2_s[k * tc:(k + 1) * tc, :])
        r_lo = r_lo + lo * w[:, k:k + 1]
        r_hi = r_hi + hi * w[:, k:k + 1]
    m = shared + jnp.concatenate([r_lo, r_hi], axis=1)
    gate2 = mod_ref[0, :, 5 * d:6 * d]
    out = _ln(alpha * x1_ref[...] + gate2 * m) * lnw_ref[...] + lnb_ref[...]

    def store(o_ref):
        o_ref[...] = out

    rows.when_prompt(i, lambda: store(op_ref))
    rows.when_latent(i, lambda: store(os_ref))


def _combine(pos, wts, hu, x1, mod3, ln_w, ln_b, wg, wu, wd, ybuf, rows, n_prompt_rows, alpha):
    t, d = x1.shape
    tc = rows.tile
    const = lambda a: pl.BlockSpec(a.shape, lambda i: (0, 0))
    return pl.pallas_call(
        functools.partial(_combine_kernel, d=d, alpha=alpha, rows=rows),
        grid=(rows.n_tiles,),
        in_specs=[
            pl.BlockSpec((TOP_K, tc), lambda i: (0, i), memory_space=pltpu.SMEM),
            pl.BlockSpec((tc, TOP_K), lambda i: (i, 0)),
            pl.BlockSpec((tc, d // 2), lambda i: (i, 0)),
            pl.BlockSpec((tc, d), lambda i: (i, 0)),
            pl.BlockSpec((1, 1, mod3.shape[2]), lambda i: (rows.mod_row(i), 0, 0)),
            const(ln_w), const(ln_b), const(wg), const(wu), const(wd),
            pl.BlockSpec(memory_space=pl.ANY),
        ],
        out_specs=[pl.BlockSpec((tc, d), lambda i: (rows.prompt_tile(i), 0)),
                   pl.BlockSpec((tc, d), lambda i: (rows.latent_tile(i), 0))],
        out_shape=[jax.ShapeDtypeStruct((n_prompt_rows, d), F32),
                   jax.ShapeDtypeStruct((t - n_prompt_rows, d), F32)],
        scratch_shapes=[pltpu.VMEM((TOP_K * tc, 1, d // 2), U32), pltpu.VMEM((TOP_K * tc, d // 2), U32),
                        pltpu.SemaphoreType.DMA],
        compiler_params=_params("arbitrary"),
        name="combine",
    )(pos, wts, hu, x1, mod3, ln_w, ln_b, wg, wu, wd, ybuf)


def kernel(x_prompt, x_sample, c, state_ret, c_ctx, w_ada, b_ada, w_in, ret_decay, ret_gn_w, w_out,
           ln1_w, ln1_b, w_router, router_bias, w_gate_e, w_up_e, w_down_e, w_gate_s, w_up_s, w_down_s,
           ln2_w, ln2_b):
    batch, seq, d = x_prompt.shape
    dec_batch, dec_seq, _ = x_sample.shape
    depth = w_in.shape[0]
    assert depth == 1
    n_prompt_rows = batch * seq
    n_latent_rows = dec_batch * dec_seq
    t = n_prompt_rows + n_latent_rows
    alpha = (2.0 * depth) ** 0.25
    d_ret = H_RET * CHUNK
    fg = (w_in.shape[2] - 4 * d_ret) // N_FGROUPS
    l = 0
    tiling = lambda tile: _Rows(tile, n_prompt_rows, n_latent_rows, dec_seq)

    xp = x_prompt.reshape(n_prompt_rows, d)
    xs = x_sample.reshape(n_latent_rows, d)

    n_mod = 1 + dec_batch
    cc = jnp.concatenate([c_ctx[None, :], c, jnp.zeros((-n_mod % 8, d), F32)], axis=0)
    mod = _ada(cc, w_ada[l], b_ada[l][None, :])
    mod3 = mod.reshape(mod.shape[0], 1, mod.shape[1])

    z = _inproj(xp, xs, mod3, w_in[l], tiling(ROW_TILE))

    log_g = jax.nn.log_sigmoid(ret_decay[l].astype(F32))
    gn_w = ret_gn_w[l][None, :]
    latent_row_block = n_prompt_rows // dec_seq
    ret_p, ctx_state = _retention(z, log_g, gn_w, batch=batch, seq=seq, row_block_offset=0, latent=False)
    ret_s = _retention(z, log_g, gn_w, batch=dec_batch, seq=dec_seq, row_block_offset=latent_row_block,
                       latent=True, state=state_ret[:, l])[0]
    fcol = 4 * d_ret // fg
    four_p = _fourier(z, batch=batch, seq=seq, row_block_offset=0, col_block_offset=fcol, fg=fg)
    four_s = _fourier(z, batch=dec_batch, seq=dec_seq, row_block_offset=latent_row_block,
                      col_block_offset=fcol, fg=fg)

    x1, hu, logits_t = _outproj(ret_p, ret_s, four_p, four_s, _cast_bf16(w_out[l]), xp, xs, mod3,
                                ln1_w[l][None, :], ln1_b[l][None, :], w_router[l].T,
                                tiling(OUT_ROW_TILE), alpha)

    eidx, wts, rank, counts = _router(logits_t, router_bias[l][:, None])
    meta, start = _expert_schedule(counts[:, 0].astype(I32), t * TOP_K)
    pos = _positions(start, eidx, rank)

    xsorted = _dispatch(pos, hu)
    ybuf = _experts(meta, xsorted, w_gate_e[l], w_up_e[l], w_down_e[l])
    y_prompt, y_sample = _combine(pos, wts.T, hu, x1, mod3, ln2_w[l][None, :], ln2_b[l][None, :],
                                  _cast_bf16(w_gate_s[l]), _cast_bf16(w_up_s[l]), _cast_bf16(w_down_s[l]),
                                  ybuf, tiling(COMBINE_TILE), n_prompt_rows, alpha)

    new_state = ctx_state[:, None].astype(x_prompt.dtype)
    return (y_prompt.reshape(batch, seq, d), y_sample.reshape(dec_batch, dec_seq, d), new_state)
```
